```python
import math
import numpy as np
import jax
import jax.numpy as jnp
from jax import lax

D_MODEL = 2048
BATCH = 16
SEQ = 256
DEPTH = 2
DEC_BATCH = 4
DEC_SEQ = 2048
PAST_LEN = 512

GRID_W = 64
CHUNK = 64
D_FF = 5632
N_MOD = 9
NORM_EPS = 1e-6
MIX_OUT = 2048
STATE_INIT_SCALE = 0.3
ROPE_BASE = 10000.0

GLA_HEADS = 4
GLA_DK = 128
GLA_DV = 256
GLA_LOWRANK = 16
GLA_GATE_NORM = 16.0
GLA_QK = GLA_HEADS * GLA_DK
GLA_V = GLA_HEADS * GLA_DV

RET_HEADS = 4
RET_DK = 128
RET_DV = 256
RET_QK = RET_HEADS * RET_DK
RET_V = RET_HEADS * RET_DV
RET_DECAY_EXP_FWD = 5.0
RET_DECAY_EXP_BWD = 5.5

GDN_HEADS = 8
GDN_DK = 128
GDN_DV = 128
GDN_QK = GDN_HEADS * GDN_DK
GDN_V = GDN_HEADS * GDN_DV
GDN_QKV = 2 * GDN_QK + GDN_V
CONV_K = 5
GDN_IN = GDN_QKV + GDN_V + 4 * GDN_HEADS

RWKV_HEADS = 16
RWKV_N = 64
RWKV_C = RWKV_HEADS * RWKV_N
RWKV_DECAY_LORA = 64
RWKV_AAA_LORA = 64
RWKV_GATE_LORA = 128
RWKV_GN_EPS = 64e-5
RWKV_IN = 3 * RWKV_C + 2 * RWKV_DECAY_LORA + 2 * RWKV_AAA_LORA + RWKV_GATE_LORA

L0_SIZES = (GLA_QK, GLA_QK, GLA_V, GLA_V, GLA_LOWRANK, GLA_LOWRANK, RET_QK, RET_QK, RET_V, RET_V)
L0_IN = 2 * GLA_QK + 2 * GLA_V + 2 * GLA_LOWRANK + 2 * RET_QK + 2 * RET_V
GDN_REST_SIZES = (GDN_V, GDN_HEADS, GDN_HEADS, GDN_HEADS, GDN_HEADS)
RWKV_SIZES = (RWKV_C, RWKV_C, RWKV_C, RWKV_DECAY_LORA, RWKV_DECAY_LORA, RWKV_AAA_LORA, RWKV_AAA_LORA, RWKV_GATE_LORA)
L1_IN = GDN_IN + RWKV_IN

kernel_name = 'bidir_hybrid_gla_ret_gdn_rwkv7_prefix_dit'


def split_sizes(z, sizes):
    return jnp.split(z, np.cumsum(sizes)[:-1].tolist(), axis=-1)


def rmsnorm(x, g):
    x32 = x.astype(jnp.float32)
    y = x32 * lax.rsqrt(jnp.mean(x32 * x32, axis=-1, keepdims=True) + NORM_EPS) * g
    return y.astype(x.dtype)


def modulate(x, g, shift, scale):
    return rmsnorm(x, g) * (1 + scale) + shift


def adaln(cond, w_mod, b_mod, dtype):
    m = jax.nn.silu(cond.astype(jnp.float32)) @ w_mod + b_mod
    return [t[:, None, :].astype(dtype) for t in jnp.split(m, N_MOD, axis=-1)]


def swiglu(h, wg, wu, wd):
    return (jax.nn.silu(h @ wg) * (h @ wu)) @ wd


def heads(x, n):
    b, t, _ = x.shape
    return x.reshape(b, t, n, -1).transpose(0, 2, 1, 3)


def head_scalars(x):
    return jnp.swapaxes(x, 1, 2)


def merge_heads(x):
    b, h, t, d = x.shape
    return x.transpose(0, 2, 1, 3).reshape(b, t, h * d)


def head_norm(x, eps, center):
    if center:
        x = x - jnp.mean(x, axis=-1, keepdims=True)
    return x * lax.rsqrt(jnp.mean(x * x, axis=-1, keepdims=True) + eps)


def l2norm(x):
    return x * lax.rsqrt(jnp.sum(x * x, axis=-1, keepdims=True) + NORM_EPS)


def to_chunks(x):
    b, h, t = x.shape[:3]
    return jnp.moveaxis(x.reshape((b, h, t // CHUNK, CHUNK) + x.shape[3:]), 2, 0)


def from_chunks(y):
    y = jnp.moveaxis(y, 0, 2)
    b, h, n, c = y.shape[:4]
    return y.reshape((b, h, n * c) + y.shape[4:])


def causal_masks():
    lower = jnp.tril(jnp.ones((CHUNK, CHUNK), dtype=bool))
    strict = jnp.tril(jnp.ones((CHUNK, CHUNK), dtype=bool), -1)
    return lower, strict


def gla_chunk_scan(q, k, v, log_a, s0):
    lower, _ = causal_masks()

    def step(S, inp):
        qc, kc, vc, lc = inp
        G = jnp.cumsum(lc, axis=-2)
        rel = jnp.exp(jnp.where(lower[:, :, None], G[..., :, None, :] - G[..., None, :, :], -jnp.inf))
        att = jnp.einsum('bhik,bhjk,bhijk->bhij', qc, kc, rel)
        o = att @ vc + (qc * jnp.exp(G)) @ S
        G_end = G[..., -1:, :]
        S = jnp.swapaxes(jnp.exp(G_end), -1, -2) * S + jnp.swapaxes(kc * jnp.exp(G_end - G), -1, -2) @ vc
        return S, o

    S, o = lax.scan(step, s0.astype(jnp.float32), tuple(to_chunks(t) for t in (q, k, v, log_a)))
    return from_chunks(o), S


def retention_chunk_scan(q, k, v, s0, log_g):
    lower, _ = causal_masks()
    idx = jnp.arange(CHUNK, dtype=jnp.float32)
    lg = log_g[:, None, None]
    dist = jnp.where(lower, idx[:, None] - idx[None, :], 0.0)
    dmat = jnp.where(lower, jnp.exp(dist * lg), 0.0)
    q_dec = jnp.exp((idx + 1.0)[:, None] * lg)
    k_dec = jnp.exp((CHUNK - 1.0 - idx)[:, None] * lg)
    c_dec = jnp.exp(CHUNK * lg)

    def step(S, inp):
        qc, kc, vc = inp
        o = ((qc @ jnp.swapaxes(kc, -1, -2)) * dmat) @ vc + (qc * q_dec) @ S
        S = c_dec * S + jnp.swapaxes(kc * k_dec, -1, -2) @ vc
        return S, o

    S, o = lax.scan(step, s0.astype(jnp.float32), tuple(to_chunks(t) for t in (q, k, v)))
    return from_chunks(o), S


def gdn_chunk_scan(q, k, v, log_a, beta, s0):
    lower, strict = causal_masks()
    eye = jnp.eye(CHUNK, dtype=jnp.float32)
    dv = v.shape[-1]

    def step(S, inp):
        qc, kc, vc, lc, bc = inp
        G = jnp.cumsum(lc, axis=-1)
        rel = jnp.exp(jnp.where(lower, G[..., :, None] - G[..., None, :], -jnp.inf))
        kkt = kc @ jnp.swapaxes(kc, -1, -2)
        lmat = jnp.where(strict, bc[..., :, None] * rel * kkt, 0.0) + eye
        rhs = jnp.concatenate([bc[..., None] * vc, (bc * jnp.exp(G))[..., None] * kc], axis=-1)
        sol = lax.linalg.triangular_solve(lmat, rhs, left_side=True, lower=True)
        u = sol[..., :dv] - sol[..., dv:] @ S
        o = (qc * jnp.exp(G)[..., None]) @ S + ((qc @ jnp.swapaxes(kc, -1, -2)) * rel) @ u
        S = (jnp.exp(G[..., -1])[..., None, None] * S
             + jnp.swapaxes(kc * jnp.exp(G[..., -1:] - G)[..., None], -1, -2) @ u)
        return S, o

    S, o = lax.scan(step, s0.astype(jnp.float32), tuple(to_chunks(t) for t in (q, k, v, log_a, beta)))
    return from_chunks(o), S


def rwkv7_scan(r, log_w, k, v, a, b, s0):
    def step(S, inp):
        r_t, lw_t, k_t, v_t, a_t, b_t = inp
        sa = jnp.einsum('bhk,bhkv->bhv', a_t, S)
        S = jnp.exp(lw_t)[..., None] * S + b_t[..., :, None] * sa[..., None, :] + k_t[..., :, None] * v_t[..., None, :]
        return S, jnp.einsum('bhk,bhkv->bhv', r_t, S)

    xs = tuple(jnp.moveaxis(t, 2, 0) for t in (r, log_w, k, v, a, b))
    S, y = lax.scan(step, s0.astype(jnp.float32), xs)
    return jnp.moveaxis(y, 0, 2), S


def bidir(scan_fn, args_f, args_b, s0_f, s0_b, consts_f=(), consts_b=()):
    o_f, s_f = scan_fn(*args_f, s0_f, *consts_f)
    o_b, s_b = scan_fn(*(jnp.flip(t, axis=2) for t in args_b), s0_b, *consts_b)
    return o_f + jnp.flip(o_b, axis=2), s_f, s_b


def grid_rotary(x):
    t, dk = x.shape[2], x.shape[3]
    rows = t // GRID_W
    row = jnp.broadcast_to(jnp.arange(rows, dtype=jnp.float32)[:, None], (rows, GRID_W)).reshape(t)
    col = jnp.broadcast_to(jnp.arange(GRID_W, dtype=jnp.float32)[None, :], (rows, GRID_W)).reshape(t)
    quarter = dk // 4
    inv = ROPE_BASE ** (-jnp.arange(quarter, dtype=jnp.float32) / quarter)
    ang = jnp.concatenate([row[:, None] * inv, col[:, None] * inv], axis=-1)
    cos, sin = jnp.cos(ang), jnp.sin(ang)
    x1, x2 = x[..., :dk // 2], x[..., dk // 2:]
    return jnp.concatenate([x1 * cos - x2 * sin, x1 * sin + x2 * cos], axis=-1)


def retention_log_decay(exp0):
    h = jnp.arange(RET_HEADS, dtype=jnp.float32)
    return jnp.log1p(-jnp.power(2.0, -(exp0 + h)))


def centred_dwconv(x, w):
    return lax.conv_general_dilated(
        x, w[:, None, :].astype(x.dtype), window_strides=(1,),
        padding=((CONV_K // 2, CONV_K // 2),), dimension_numbers=('NWC', 'WIO', 'NWC'),
        feature_group_count=x.shape[-1])


def centred_shift_mix(z, mu):
    zp = jnp.pad(z, ((0, 0), (1, 1), (0, 0)))
    shifted = 0.5 * (zp[:, :-2] + zp[:, 2:])
    return z + mu * (shifted - z)


def mixer_gla_ret(h, states, latent, w_in, w_out, gk_up_f, gk_b_f, gk_up_b, gk_b_b, gla_norm, ret_norm):
    b = h.shape[0]
    z = (h @ w_in).astype(jnp.float32)
    gq, gk, gv, gg, gdf, gdb, rq, rk, rv, rg = split_sizes(z, L0_SIZES)
    if states is None:
        zg = jnp.zeros((b, GLA_HEADS, GLA_DK, GLA_DV), jnp.float32)
        zr = jnp.zeros((b, RET_HEADS, RET_DK, RET_DV), jnp.float32)
        states = (zg, zg, zr, zr)
    q = heads(gq, GLA_HEADS) * GLA_DK ** -0.5
    k = heads(gk, GLA_HEADS)
    v = heads(gv, GLA_HEADS)
    la_f = heads(jax.nn.log_sigmoid(gdf @ gk_up_f + gk_b_f) / GLA_GATE_NORM, GLA_HEADS)
    la_b = heads(jax.nn.log_sigmoid(gdb @ gk_up_b + gk_b_b) / GLA_GATE_NORM, GLA_HEADS)
    o, s_gf, s_gb = bidir(gla_chunk_scan, (q, k, v, la_f), (q, k, v, la_b), states[0], states[1])
    o_gla = merge_heads(head_norm(o, NORM_EPS, False) * gla_norm) * jax.nn.silu(gg)
    q = heads(rq, RET_HEADS)
    k = heads(rk, RET_HEADS)
    if latent:
        q, k = grid_rotary(q), grid_rotary(k)
    q = q * RET_DK ** -0.5
    v = heads(rv, RET_HEADS)
    o, s_rf, s_rb = bidir(retention_chunk_scan, (q, k, v), (q, k, v), states[2], states[3],
                          (retention_log_decay(RET_DECAY_EXP_FWD),), (retention_log_decay(RET_DECAY_EXP_BWD),))
    o_ret = merge_heads(head_norm(o, NORM_EPS, True) * ret_norm) * jax.nn.silu(rg)
    y = jnp.concatenate([o_gla, o_ret], axis=-1).astype(h.dtype) @ w_out
    return y, (s_gf, s_gb, s_rf, s_rb)


def mixer_gdn_rwkv(h, states, w_in, w_out, conv_w, A_log_f, dt_bias_f, A_log_b, dt_bias_b, gdn_norm,
                   mu, w0_f, w2_f, a0_f, a2_f, w0_b, w2_b, a0_b, a2_b, g2, k_k, k_a, r_k, ln_w, ln_b):
    b = h.shape[0]
    z = (h @ w_in).astype(jnp.float32)
    z_gdn, z_rwkv = z[..., :GDN_IN], z[..., GDN_IN:]
    if states is None:
        zd = jnp.zeros((b, GDN_HEADS, GDN_DK, GDN_DV), jnp.float32)
        zw = jnp.zeros((b, RWKV_HEADS, RWKV_N, RWKV_N), jnp.float32)
        states = (zd, zd, zw, zw)
    qkv = jax.nn.silu(centred_dwconv(z_gdn[..., :GDN_QKV], conv_w))
    gq, gk, gv = split_sizes(qkv, (GDN_QK, GDN_QK, GDN_V))
    gg, a_f, a_b, b_f, b_b = split_sizes(z_gdn[..., GDN_QKV:], GDN_REST_SIZES)
    q = l2norm(heads(gq, GDN_HEADS)) * GDN_DK ** -0.5
    k = l2norm(heads(gk, GDN_HEADS))
    v = heads(gv, GDN_HEADS)
    la_f = -jnp.exp(A_log_f)[:, None] * jax.nn.softplus(head_scalars(a_f) + dt_bias_f[:, None])
    la_b = -jnp.exp(A_log_b)[:, None] * jax.nn.softplus(head_scalars(a_b) + dt_bias_b[:, None])
    beta_f = jax.nn.sigmoid(head_scalars(b_f))
    beta_b = jax.nn.sigmoid(head_scalars(b_b))
    o, s_df, s_db = bidir(gdn_chunk_scan, (q, k, v, la_f, beta_f), (q, k, v, la_b, beta_b), states[0], states[1])
    o_gdn = merge_heads(head_norm(o, NORM_EPS, False) * gdn_norm) * jax.nn.silu(gg)
    zr = centred_shift_mix(z_rwkv, mu)
    r, kr, vr, wd_f, wd_b, ad_f, ad_b, gd = split_sizes(zr, RWKV_SIZES)
    rh = heads(r, RWKV_HEADS)
    vh = heads(vr, RWKV_HEADS)
    kk = l2norm(heads(kr * k_k, RWKV_HEADS))

    def rwkv_direction(w0, wd, w2, a0, ad, a2):
        w = -jax.nn.softplus(-(w0 + jnp.tanh(wd) @ w2)) - 0.5
        a = jax.nn.sigmoid(a0 + ad @ a2)
        kd = heads(kr * (1.0 + (a - 1.0) * k_a), RWKV_HEADS)
        bonus = jnp.sum(rh * kd * r_k[:, None, :], axis=-1, keepdims=True) * vh
        args = (rh, -jnp.exp(heads(w, RWKV_HEADS)), kd, vh, -kk, kk * heads(a, RWKV_HEADS))
        return args, bonus

    args_f, bonus_f = rwkv_direction(w0_f, wd_f, w2_f, a0_f, ad_f, a2_f)
    args_b, bonus_b = rwkv_direction(w0_b, wd_b, w2_b, a0_b, ad_b, a2_b)
    o, s_wf, s_wb = bidir(rwkv7_scan, args_f, args_b, states[2], states[3])
    y_rwkv = merge_heads(head_norm(o, RWKV_GN_EPS, True)) * ln_w + ln_b + merge_heads(bonus_f + bonus_b)
    y_rwkv = y_rwkv * (jax.nn.sigmoid(gd) @ g2)
    y = jnp.concatenate([o_gdn, y_rwkv], axis=-1).astype(h.dtype) @ w_out
    return y, (s_df, s_db, s_wf, s_wb)


def setup_inputs(seed: int = 0) -> dict:
    key = jax.random.key(seed)
    ks = iter(jax.random.split(key, 128))

    def nrm(shape, scale=1.0):
        return scale * jax.random.normal(next(ks), shape, jnp.float32)

    def gain(n):
        return 1.0 + 0.05 * nrm((n,))

    def unif(shape, lo, hi):
        return jax.random.uniform(next(ks), shape, jnp.float32, lo, hi)

    D = D_MODEL
    p = {}
    p['x_prompt'] = nrm((BATCH, SEQ, D))
    p['x_sample'] = nrm((DEC_BATCH, DEC_SEQ, D))
    p['c'] = nrm((DEC_BATCH, D))
    p['c_ctx'] = nrm((D,))
    state_shapes = (('l0_gla', (GLA_HEADS, GLA_DK, GLA_DV)), ('l0_ret', (RET_HEADS, RET_DK, RET_DV)),
                    ('l1_gdn', (GDN_HEADS, GDN_DK, GDN_DV)), ('l1_rwkv', (RWKV_HEADS, RWKV_N, RWKV_N)))
    for name, shp in state_shapes:
        for d in ('fwd', 'bwd'):
            p['state_' + name + '_' + d] = nrm((DEC_BATCH,) + shp, STATE_INIT_SCALE)

    def add_common(l, n_in):
        pre = 'l' + str(l) + '_'
        p[pre + 'w_mod'] = nrm((D, N_MOD * D), 0.5 * D ** -0.5)
        p[pre + 'b_mod'] = nrm((N_MOD * D,), 0.02)
        for i in (1, 2, 3):
            p[pre + 'norm' + str(i)] = gain(D)
        for f in ('ffn1', 'ffn2'):
            p[pre + f + '_wg'] = nrm((D, D_FF), D ** -0.5)
            p[pre + f + '_wu'] = nrm((D, D_FF), D ** -0.5)
            p[pre + f + '_wd'] = nrm((D_FF, D), D_FF ** -0.5)
        p[pre + 'w_in'] = nrm((D, n_in), D ** -0.5)
        p[pre + 'w_out'] = nrm((MIX_OUT, D), MIX_OUT ** -0.5)

    add_common(0, L0_IN)
    for d in ('fwd', 'bwd'):
        p['l0_gla_gk_up_' + d] = nrm((GLA_LOWRANK, GLA_QK), GLA_LOWRANK ** -0.5)
        p['l0_gla_gk_b_' + d] = nrm((GLA_QK,), 0.1)
    p['l0_gla_norm'] = gain(GLA_DV)
    p['l0_ret_norm'] = gain(RET_DV)

    add_common(1, L1_IN)
    p['l1_gdn_conv'] = nrm((CONV_K, GDN_QKV), CONV_K ** -0.5)
    for d in ('fwd', 'bwd'):
        p['l1_gdn_A_log_' + d] = jnp.log(unif((GDN_HEADS,), 1.0, 16.0))
        dt = jnp.exp(unif((GDN_HEADS,), math.log(1e-3), math.log(1e-1)))
        p['l1_gdn_dt_bias_' + d] = dt + jnp.log(-jnp.expm1(-dt))
    p['l1_gdn_norm'] = gain(GDN_DV)
    p['l1_rwkv_mu'] = unif((RWKV_IN,), 0.0, 1.0)
    for d in ('fwd', 'bwd'):
        p['l1_rwkv_w0_' + d] = -1.0 + 0.5 * nrm((RWKV_C,))
        p['l1_rwkv_w2_' + d] = nrm((RWKV_DECAY_LORA, RWKV_C), 0.1)
        p['l1_rwkv_a0_' + d] = nrm((RWKV_C,), 0.1)
        p['l1_rwkv_a2_' + d] = nrm((RWKV_AAA_LORA, RWKV_C), RWKV_AAA_LORA ** -0.5)
    p['l1_rwkv_g2'] = nrm((RWKV_GATE_LORA, RWKV_C), RWKV_GATE_LORA ** -0.5)
    p['l1_rwkv_k_k'] = 0.85 + 0.05 * nrm((RWKV_C,))
    p['l1_rwkv_k_a'] = 1.0 + 0.05 * nrm((RWKV_C,))
    p['l1_rwkv_r_k'] = nrm((RWKV_HEADS, RWKV_N), 0.1)
    p['l1_rwkv_ln_w'] = gain(RWKV_C)
    p['l1_rwkv_ln_b'] = nrm((RWKV_C,), 0.02)
    p['final_norm'] = gain(D)
    return p


def reference(x_prompt, x_sample, c, c_ctx,
              state_l0_gla_fwd, state_l0_gla_bwd, state_l0_ret_fwd, state_l0_ret_bwd,
              state_l1_gdn_fwd, state_l1_gdn_bwd, state_l1_rwkv_fwd, state_l1_rwkv_bwd,
              l0_w_mod, l0_b_mod, l0_norm1, l0_norm2, l0_norm3,
              l0_ffn1_wg, l0_ffn1_wu, l0_ffn1_wd, l0_ffn2_wg, l0_ffn2_wu, l0_ffn2_wd, l0_w_in, l0_w_out,
              l0_gla_gk_up_fwd, l0_gla_gk_b_fwd, l0_gla_gk_up_bwd, l0_gla_gk_b_bwd, l0_gla_norm, l0_ret_norm,
              l1_w_mod, l1_b_mod, l1_norm1, l1_norm2, l1_norm3,
              l1_ffn1_wg, l1_ffn1_wu, l1_ffn1_wd, l1_ffn2_wg, l1_ffn2_wu, l1_ffn2_wd, l1_w_in, l1_w_out,
              l1_gdn_conv, l1_gdn_A_log_fwd, l1_gdn_dt_bias_fwd, l1_gdn_A_log_bwd, l1_gdn_dt_bias_bwd, l1_gdn_norm,
              l1_rwkv_mu, l1_rwkv_w0_fwd, l1_rwkv_w2_fwd, l1_rwkv_a0_fwd, l1_rwkv_a2_fwd,
              l1_rwkv_w0_bwd, l1_rwkv_w2_bwd, l1_rwkv_a0_bwd, l1_rwkv_a2_bwd,
              l1_rwkv_g2, l1_rwkv_k_k, l1_rwkv_k_a, l1_rwkv_r_k, l1_rwkv_ln_w, l1_rwkv_ln_b,
              final_norm):
    common = (
        (l0_w_mod, l0_b_mod, (l0_norm1, l0_norm2, l0_norm3), (l0_ffn1_wg, l0_ffn1_wu, l0_ffn1_wd),
         (l0_ffn2_wg, l0_ffn2_wu, l0_ffn2_wd), l0_w_in, l0_w_out),
        (l1_w_mod, l1_b_mod, (l1_norm1, l1_norm2, l1_norm3), (l1_ffn1_wg, l1_ffn1_wu, l1_ffn1_wd),
         (l1_ffn2_wg, l1_ffn2_wu, l1_ffn2_wd), l1_w_in, l1_w_out),
    )
    mixer_params = (
        (l0_gla_gk_up_fwd, l0_gla_gk_b_fwd, l0_gla_gk_up_bwd, l0_gla_gk_b_bwd, l0_gla_norm, l0_ret_norm),
        (l1_gdn_conv, l1_gdn_A_log_fwd, l1_gdn_dt_bias_fwd, l1_gdn_A_log_bwd, l1_gdn_dt_bias_bwd, l1_gdn_norm,
         l1_rwkv_mu, l1_rwkv_w0_fwd, l1_rwkv_w2_fwd, l1_rwkv_a0_fwd, l1_rwkv_a2_fwd,
         l1_rwkv_w0_bwd, l1_rwkv_w2_bwd, l1_rwkv_a0_bwd, l1_rwkv_a2_bwd,
         l1_rwkv_g2, l1_rwkv_k_k, l1_rwkv_k_a, l1_rwkv_r_k, l1_rwkv_ln_w, l1_rwkv_ln_b),
    )
    caches = (
        (state_l0_gla_fwd, state_l0_gla_bwd, state_l0_ret_fwd, state_l0_ret_bwd),
        (state_l1_gdn_fwd, state_l1_gdn_bwd, state_l1_rwkv_fwd, state_l1_rwkv_bwd),
    )
    xp, xs = x_prompt, x_sample
    new_states = []
    for layer in range(DEPTH):
        w_mod, b_mod, norms, ffn1, ffn2, w_in, w_out = common[layer]
        outs = []
        for x, cond, cache, latent in ((xp, c_ctx[None, :], None, False), (xs, c, caches[layer], True)):
            sh1, sc1, g1, sh2, sc2, g2, sh3, sc3, g3 = adaln(cond, w_mod, b_mod, x.dtype)
            x = x + 0.5 * g1 * swiglu(modulate(x, norms[0], sh1, sc1), *ffn1)
            hm = modulate(x, norms[1], sh2, sc2)
            if layer % 2 == 0:
                m, st = mixer_gla_ret(hm, cache, latent, w_in, w_out, *mixer_params[layer])
            else:
                m, st = mixer_gdn_rwkv(hm, cache, w_in, w_out, *mixer_params[layer])
            x = x + g2 * m
            x = x + 0.5 * g3 * swiglu(modulate(x, norms[2], sh3, sc3), *ffn2)
            outs.append((x, st))
        (xp, st_prompt), (xs, _) = outs
        new_states.extend(st_prompt)
    y_prompt = rmsnorm(xp, final_norm)
    y_sample = rmsnorm(xs, final_norm)
    return (y_prompt, y_sample, *new_states)
```

```python
import functools
import math

import numpy as np
import jax
import jax.numpy as jnp
from jax import lax
from jax.experimental import pallas as pl
from jax.experimental.pallas import tpu as pltpu

F32 = jnp.float32
BF16 = jnp.bfloat16

D_MODEL = 2048
D_FF = 5632
N_MOD = 9
NORM_EPS = 1e-6
GRID_W = 64
ROPE_BASE = 10000.0
CHUNK = 64
SUB = 16

GLA_HEADS, GLA_DK, GLA_DV, GLA_LOWRANK, GLA_GATE_NORM = 4, 128, 256, 16, 16.0
GLA_QK, GLA_V = GLA_HEADS * GLA_DK, GLA_HEADS * GLA_DV
RET_HEADS, RET_DK, RET_DV = 4, 128, 256
RET_QK, RET_V = RET_HEADS * RET_DK, RET_HEADS * RET_DV
RET_DECAY_EXP_FWD, RET_DECAY_EXP_BWD = 5.0, 5.5
GDN_HEADS, GDN_DK, GDN_DV = 8, 128, 128
GDN_QK, GDN_V = GDN_HEADS * GDN_DK, GDN_HEADS * GDN_DV
GDN_QKV = 2 * GDN_QK + GDN_V
CONV_K = 5
GDN_IN = GDN_QKV + GDN_V + 4 * GDN_HEADS
RWKV_HEADS, RWKV_N = 16, 64
RWKV_C = RWKV_HEADS * RWKV_N
RWKV_DECAY_LORA, RWKV_AAA_LORA, RWKV_GATE_LORA = 64, 64, 128
RWKV_GN_EPS = 64e-5
RWKV_IN = 3 * RWKV_C + 2 * RWKV_DECAY_LORA + 2 * RWKV_AAA_LORA + RWKV_GATE_LORA
L0_SIZES = (GLA_QK, GLA_QK, GLA_V, GLA_V, GLA_LOWRANK, GLA_LOWRANK, RET_QK, RET_QK, RET_V, RET_V)
GDN_REST_SIZES = (GDN_V, GDN_HEADS, GDN_HEADS, GDN_HEADS, GDN_HEADS)
RWKV_SIZES = (RWKV_C, RWKV_C, RWKV_C, RWKV_DECAY_LORA, RWKV_DECAY_LORA, RWKV_AAA_LORA, RWKV_AAA_LORA,
              RWKV_GATE_LORA)

VMEM_LIMIT_BYTES = 56 * 1024 * 1024


def _split_sizes(z, sizes):
    return jnp.split(z, np.cumsum(sizes)[:-1].tolist(), axis=-1)


def _bdot(a, b):
    return jnp.dot(a.astype(BF16), b.astype(BF16), preferred_element_type=F32)


def _bdot_nt(a, b):
    return lax.dot_general(a.astype(BF16), b.astype(BF16), (((1,), (1,)), ((), ())),
                           preferred_element_type=F32)


def _bdot_tn(a, b):
    return lax.dot_general(a.astype(BF16), b.astype(BF16), (((0,), (0,)), ((), ())),
                           preferred_element_type=F32)


def _split2(x):
    hi = x.astype(BF16)
    lo = (x - hi.astype(F32)).astype(BF16)
    return hi, lo


def _dot3(a, b):
    ah, al = _split2(a)
    bh, bl = _split2(b)
    d = lambda x, y: jnp.dot(x, y, preferred_element_type=F32)
    return d(ah, bh) + (d(ah, bl) + d(al, bh))


def _cumsum_rows(tri, x):
    x1 = x.astype(BF16)
    r1 = x - x1.astype(F32)
    x2 = r1.astype(BF16)
    x3 = (r1 - x2.astype(F32)).astype(BF16)
    d = lambda y: jnp.dot(tri, y, preferred_element_type=F32)
    return d(x1) + (d(x2) + d(x3))


def _iota2(n, m, axis):
    return lax.broadcasted_iota(jnp.int32, (n, m), axis)


def _tri_inverse(n_mat, eye):
    p = eye - n_mat
    m = n_mat
    for _ in range(5):
        m = _dot3(m, m)
        p = p + _dot3(p, m)
    return p


def _row_to_col(row_vec, eye):
    return jnp.sum(jnp.where(eye, row_vec, 0.0), axis=1, keepdims=True)


def _gla_chunk(q, k, v, lc, s, rev):
    c, kd = q.shape
    row = _iota2(c, c, 0)
    col = _iota2(c, c, 1)
    incl = (row >= col) if not rev else (row <= col)
    tri = jnp.where(incl, 1.0, 0.0).astype(BF16)
    g = _cumsum_rows(tri, lc)
    ns = c // SUB
    rows = []
    for i in range(ns):
        first = (i == 0) if not rev else (i == ns - 1)
        if first:
            rows.append(jnp.zeros((SUB, c), F32))
            continue
        r0 = i * SUB
        gref = g[r0 - 1:r0, :] if not rev else g[r0 + SUB:r0 + SUB + 1, :]
        qs = q[r0:r0 + SUB, :] * jnp.exp(g[r0:r0 + SUB, :] - gref)
        ks = k * jnp.exp(jnp.minimum(gref - g, 0.0))
        rows.append(_bdot_nt(qs, ks))
    att = jnp.concatenate(rows, axis=0)
    blk_r = jnp.right_shift(row, 4)
    blk_c = jnp.right_shift(col, 4)
    att = jnp.where((blk_r > blk_c) if not rev else (blk_r < blk_c), att, 0.0)
    dist = (row - col) if not rev else (col - row)
    dist = jnp.where(blk_r == blk_c, dist, -1)
    for d in range(SUB):
        if d == 0:
            ksh, gsh = k, g
        else:
            sh = d if not rev else c - d
            ksh = pltpu.roll(k, sh, 0)
            gsh = pltpu.roll(g, sh, 0)
        e = jnp.exp(jnp.minimum(g - gsh, 0.0))
        band = jnp.sum(q * ksh * e, axis=1, keepdims=True)
        att = jnp.where(dist == d, band, att)
    o = _bdot(att, v) + _bdot(q * jnp.exp(g), s)
    g_end = g[c - 1:c, :] if not rev else g[0:1, :]
    eye_k = _iota2(kd, kd, 0) == _iota2(kd, kd, 1)
    dcol = _row_to_col(jnp.exp(g_end), eye_k)
    s_new = dcol * s + _bdot_tn(k * jnp.exp(g_end - g), v)
    return o, s_new


def _scalar_chunk(q, k, v, lc_row, beta_row, s, rev, delta):
    c = q.shape[0]
    row = _iota2(c, c, 0)
    col = _iota2(c, c, 1)
    eye = row == col
    incl = (row >= col) if not rev else (row <= col)
    strict = (row > col) if not rev else (row < col)
    lc_col = _row_to_col(lc_row, eye)
    before = (row <= col) if not rev else (row >= col)
    g_row = jnp.sum(jnp.where(before, lc_col, 0.0), axis=0, keepdims=True)
    g_col = _row_to_col(g_row, eye)
    rel = jnp.where(incl, jnp.exp(jnp.minimum(g_col - g_row, 0.0)), 0.0)
    eg = jnp.exp(g_col)
    qk = _bdot_nt(q, k)
    if delta:
        beta_col = _row_to_col(beta_row, eye)
        kk = _bdot_nt(k, k)
        n_mat = jnp.where(strict, beta_col * rel * kk, 0.0)
        t_inv = _tri_inverse(n_mat, jnp.where(eye, 1.0, 0.0))
        sol_v = _bdot(t_inv, beta_col * v)
        sol_k = _bdot(t_inv, (beta_col * eg) * k)
        u = sol_v - _bdot(sol_k, s)
    else:
        u = v
    o = _bdot(q * eg, s) + _bdot(qk * rel, u)
    g_end = g_col[c - 1:c, :] if not rev else g_col[0:1, :]
    s_new = jnp.exp(g_end) * s + _bdot_tn(k * jnp.exp(g_end - g_col), u)
    return o, s_new


def _rwkv_chunk(r, lw, k, v, a, b, s, rev):
    c, n = r.shape
    row = _iota2(c, c, 0)
    col = _iota2(c, c, 1)
    eye = row == col
    incl = (row >= col) if not rev else (row <= col)
    strict = (row > col) if not rev else (row < col)
    tri = jnp.where(incl, 1.0, 0.0).astype(BF16)
    g = _cumsum_rows(tri, lw)
    gx = g - lw
    eg = jnp.exp(g)
    einv = jnp.exp(-g)
    at = a * jnp.exp(gx)
    rt = r * eg
    bt = b * einv
    kt = k * einv
    a_ab = jnp.where(strict, _bdot_nt(at, bt), 0.0)
    a_ak = jnp.where(strict, _bdot_nt(at, kt), 0.0)
    a_rb = jnp.where(incl, _bdot_nt(rt, bt), 0.0)
    a_rk = jnp.where(incl, _bdot_nt(rt, kt), 0.0)
    t_inv = _tri_inverse(-a_ab, jnp.where(eye, 1.0, 0.0))
    u = _bdot(t_inv, _bdot(at, s) + _bdot(a_ak, v))
    y = _bdot(rt, s) + _bdot(a_rb, u) + _bdot(a_rk, v)
    g_end = g[c - 1:c, :] if not rev else g[0:1, :]
    dec = jnp.exp(g_end - g)
    eye_n = _iota2(n, n, 0) == _iota2(n, n, 1)
    dcol = _row_to_col(jnp.exp(g_end), eye_n)
    s_new = dcol * s + _bdot_tn(b * dec, u) + _bdot_tn(k * dec, v)
    return y, s_new


def _scan_kernel(*refs, kind, rev, n_in, has_s0, hb, ncs, delta):
    in_refs = refs[:n_in]
    pos = n_in
    s0_ref = None
    if has_s0:
        s0_ref = refs[pos]
        pos += 1
    o_ref, sout_ref, s_scr = refs[pos], refs[pos + 1], refs[pos + 2]
    t = pl.program_id(2)
    nt = pl.num_programs(2)

    @pl.when(t == 0)
    def _():
        if has_s0:
            s_scr[...] = s0_ref[0]
        else:
            s_scr[...] = jnp.zeros(s_scr.shape, F32)

    def body(ci, carry):
        cc = (ncs - 1 - ci) if rev else ci
        off = pl.multiple_of(cc * CHUNK, CHUNK)
        for h in range(hb):
            s = s_scr[h]
            if kind == "gla":
                q_ref, k_ref, v_ref, la_ref = in_refs
                o, s_new = _gla_chunk(q_ref[0, h, pl.ds(off, CHUNK), :], k_ref[0, h, pl.ds(off, CHUNK), :],
                                      v_ref[0, h, pl.ds(off, CHUNK), :], la_ref[0, h, pl.ds(off, CHUNK), :],
                                      s, rev)
            elif kind == "scalar":
                q_ref, k_ref, v_ref, la_ref, be_ref = in_refs
                o, s_new = _scalar_chunk(q_ref[0, h, pl.ds(off, CHUNK), :], k_ref[0, h, pl.ds(off, CHUNK), :],
                                         v_ref[0, h, pl.ds(off, CHUNK), :],
                                         la_ref[0, h, 0, pl.ds(cc, 1), :], be_ref[0, h, 0, pl.ds(cc, 1), :],
                                         s, rev, delta)
            else:
                r_ref, w_ref, k_ref, v_ref, a_ref, b_ref = in_refs
                sl = (0, h, pl.ds(off, CHUNK), slice(None))
                o, s_new = _rwkv_chunk(r_ref[sl], w_ref[sl], k_ref[sl], v_ref[sl], a_ref[sl], b_ref[sl], s, rev)
            o_ref[0, h, pl.ds(off, CHUNK), :] = o
            s_scr[h] = s_new
        return carry

    lax.fori_loop(0, ncs, body, 0)

    @pl.when(t == nt - 1)
    def _():
        sout_ref[0] = s_scr[...]


def _scan_call(kind, ins, s0, rev, dv, delta=False, hb=4, tc=256):
    b, h, t, dk = ins[0].shape
    tc = min(tc, t)
    hb = min(hb, h)
    nt = t // tc
    ncs = tc // CHUNK

    def tmap(bi, hi, ti):
        return (bi, hi, (nt - 1 - ti) if rev else ti, 0)

    def rmap(bi, hi, ti):
        return (bi, hi, (nt - 1 - ti) if rev else ti, 0, 0)

    in_specs = []
    for x in ins:
        if x.ndim == 4:
            in_specs.append(pl.BlockSpec((1, hb, tc, x.shape[-1]), tmap))
        else:
            in_specs.append(pl.BlockSpec((1, hb, 1, ncs, CHUNK), rmap))
    args = list(ins)
    if s0 is not None:
        in_specs.append(pl.BlockSpec((1, hb, dk, dv), lambda bi, hi, ti: (bi, hi, 0, 0)))
        args.append(s0)
    kern = functools.partial(_scan_kernel, kind=kind, rev=rev, n_in=len(ins), has_s0=s0 is not None,
                             hb=hb, ncs=ncs, delta=delta)
    o, s_out = pl.pallas_call(
        kern,
        grid=(b, h // hb, nt),
        in_specs=in_specs,
        out_specs=[pl.BlockSpec((1, hb, tc, dv), tmap),
                   pl.BlockSpec((1, hb, dk, dv), lambda bi, hi, ti: (bi, hi, 0, 0))],
        out_shape=[jax.ShapeDtypeStruct((b, h, t, dv), F32), jax.ShapeDtypeStruct((b, h, dk, dv), F32)],
        scratch_shapes=[pltpu.VMEM((hb, dk, dv), F32)],
        compiler_params=pltpu.CompilerParams(dimension_semantics=("arbitrary", "arbitrary", "arbitrary"),
                                             vmem_limit_bytes=VMEM_LIMIT_BYTES),
        name=f"scan_{kind}_{'bwd' if rev else 'fwd'}",
    )(*args)
    return o, s_out


def _bidir_scan(kind, ins_f, ins_b, s0_f, s0_b, dv, delta=False):
    o_f, s_f = _scan_call(kind, ins_f, s0_f, False, dv, delta)
    o_b, s_b = _scan_call(kind, ins_b, s0_b, True, dv, delta)
    return o_f + o_b, s_f, s_b


def _group_of_tile(i, tm, n_prompt, t_latent):
    start = i * tm
    return jnp.where(start < n_prompt, 0, 1 + (start - n_prompt) // t_latent)


def _modulated_norm(x, nw, shift, scale):
    y = x * lax.rsqrt(jnp.mean(x * x, axis=-1, keepdims=True) + NORM_EPS) * nw
    return y * (1.0 + scale) + shift


def _adaln_kernel(c_ref, w_ref, b_ref, o_ref):
    cnd = c_ref[...]
    act = cnd * jax.nn.sigmoid(cnd)
    o_ref[...] = _bdot(act, w_ref[...]) + b_ref[...]


def _adaln(cond, w_mod, b_mod):
    r, d = cond.shape
    n = w_mod.shape[1]
    tn = 1024
    return pl.pallas_call(
        _adaln_kernel,
        grid=(n // tn,),
        in_specs=[pl.BlockSpec((r, d), lambda j: (0, 0)),
                  pl.BlockSpec((d, tn), lambda j: (0, j)),
                  pl.BlockSpec((1, tn), lambda j: (0, j))],
        out_specs=pl.BlockSpec((r, tn), lambda j: (0, j)),
        out_shape=jax.ShapeDtypeStruct((r, n), F32),
        compiler_params=pltpu.CompilerParams(dimension_semantics=("arbitrary",),
                                             vmem_limit_bytes=VMEM_LIMIT_BYTES),
        name="adaln",
    )(cond, w_mod, b_mod.reshape(1, n))


def _ffn_kernel(x_ref, sh_ref, sc_ref, g_ref, nw_ref, wg_ref, wu_ref, wd_ref, fn_ref, o_ref, h_scr, *,
                nf, final_norm):
    f = pl.program_id(1)

    @pl.when(f == 0)
    def _():
        h = _modulated_norm(x_ref[...], nw_ref[...], sh_ref[0], sc_ref[0])
        h_scr[...] = h.astype(BF16)
        o_ref[...] = jnp.zeros(o_ref.shape, F32)

    h = h_scr[...]
    gate = jnp.dot(h, wg_ref[...], preferred_element_type=F32)
    up = jnp.dot(h, wu_ref[...], preferred_element_type=F32)
    act = (gate * jax.nn.sigmoid(gate)) * up
    o_ref[...] += jnp.dot(act.astype(BF16), wd_ref[...], preferred_element_type=F32)

    @pl.when(f == nf - 1)
    def _():
        y = x_ref[...] + (0.5 * g_ref[0]) * o_ref[...]
        if final_norm:
            y = y * lax.rsqrt(jnp.mean(y * y, axis=-1, keepdims=True) + NORM_EPS) * fn_ref[...]
        o_ref[...] = y


def _ffn(x, mods, nw, wg, wu, wd, fnorm, n_prompt, t_latent, final_norm, tm=512, tf=512):
    n, d = x.shape
    shift, scale, gate = mods
    dff = wg.shape[1]
    nf = dff // tf
    grp = functools.partial(_group_of_tile, tm=tm, n_prompt=n_prompt, t_latent=t_latent)
    mod_spec = pl.BlockSpec((1, 1, d), lambda i, f: (grp(i), 0, 0))
    vec_spec = pl.BlockSpec((1, d), lambda i, f: (0, 0))
    return pl.pallas_call(
        functools.partial(_ffn_kernel, nf=nf, final_norm=final_norm),
        grid=(n // tm, nf),
        in_specs=[pl.BlockSpec((tm, d), lambda i, f: (i, 0)), mod_spec, mod_spec, mod_spec, vec_spec,
                  pl.BlockSpec((d, tf), lambda i, f: (0, f)),
                  pl.BlockSpec((d, tf), lambda i, f: (0, f)),
                  pl.BlockSpec((tf, d), lambda i, f: (f, 0)),
                  vec_spec],
        out_specs=pl.BlockSpec((tm, d), lambda i, f: (i, 0)),
        out_shape=jax.ShapeDtypeStruct((n, d), F32),
        scratch_shapes=[pltpu.VMEM((tm, d), BF16)],
        compiler_params=pltpu.CompilerParams(dimension_semantics=("arbitrary", "arbitrary"),
                                             vmem_limit_bytes=VMEM_LIMIT_BYTES),
        name="ffn",
    )(x, shift, scale, gate, nw.reshape(1, d), wg, wu, wd, fnorm.reshape(1, d))


def _proj_in_kernel(x_ref, sh_ref, sc_ref, nw_ref, w_ref, o_ref, h_scr):
    @pl.when(pl.program_id(1) == 0)
    def _():
        h_scr[...] = _modulated_norm(x_ref[...], nw_ref[...], sh_ref[0], sc_ref[0]).astype(BF16)

    o_ref[...] = jnp.dot(h_scr[...], w_ref[...], preferred_element_type=F32)


def _proj_in(x, shift, scale, nw, w, n_prompt, t_latent, tm=512, tn=512):
    n, d = x.shape
    n_out = w.shape[1]
    grp = functools.partial(_group_of_tile, tm=tm, n_prompt=n_prompt, t_latent=t_latent)
    mod_spec = pl.BlockSpec((1, 1, d), lambda i, j: (grp(i), 0, 0))
    return pl.pallas_call(
        _proj_in_kernel,
        grid=(n // tm, n_out // tn),
        in_specs=[pl.BlockSpec((tm, d), lambda i, j: (i, 0)), mod_spec, mod_spec,
                  pl.BlockSpec((1, d), lambda i, j: (0, 0)),
                  pl.BlockSpec((d, tn), lambda i, j: (0, j))],
        out_specs=pl.BlockSpec((tm, tn), lambda i, j: (i, j)),
        out_shape=jax.ShapeDtypeStruct((n, n_out), F32),
        scratch_shapes=[pltpu.VMEM((tm, d), BF16)],
        compiler_params=pltpu.CompilerParams(dimension_semantics=("arbitrary", "arbitrary"),
                                             vmem_limit_bytes=VMEM_LIMIT_BYTES),
        name="proj_in",
    )(x, shift, scale, nw.reshape(1, d), w)


def _proj_out_kernel(x_ref, u_ref, g_ref, w_ref, o_ref):
    o_ref[...] = x_ref[...] + g_ref[0] * jnp.dot(u_ref[...].astype(BF16), w_ref[...],
                                                 preferred_element_type=F32)


def _proj_out(x, u, gate, w, n_prompt, t_latent, tm=512):
    n, d = x.shape
    k = u.shape[1]
    grp = functools.partial(_group_of_tile, tm=tm, n_prompt=n_prompt, t_latent=t_latent)
    return pl.pallas_call(
        _proj_out_kernel,
        grid=(n // tm,),
        in_specs=[pl.BlockSpec((tm, d), lambda i: (i, 0)),
                  pl.BlockSpec((tm, k), lambda i: (i, 0)),
                  pl.BlockSpec((1, 1, d), lambda i: (grp(i), 0, 0)),
                  pl.BlockSpec((k, d), lambda i: (0, 0))],
        out_specs=pl.BlockSpec((tm, d), lambda i: (i, 0)),
        out_shape=jax.ShapeDtypeStruct((n, d), F32),
        compiler_params=pltpu.CompilerParams(dimension_semantics=("arbitrary",),
                                             vmem_limit_bytes=VMEM_LIMIT_BYTES),
        name="proj_out",
    )(x, u, gate, w)


def _softplus(x):
    return jnp.maximum(x, 0.0) + jnp.log1p(jnp.exp(-jnp.abs(x)))


def _lora_kernel(x_ref, w_ref, b_ref, o_ref, *, pre, post):
    x = x_ref[...]
    if pre == "tanh":
        x = jnp.tanh(x)
    elif pre == "sigmoid":
        x = jax.nn.sigmoid(x)
    y = _bdot(x, w_ref[...]) + b_ref[...]
    if post == "logsig_gla":
        y = -_softplus(-y) / GLA_GATE_NORM
    elif post == "rwkv_w":
        y = -jnp.exp(-_softplus(-y) - 0.5)
    elif post == "sigmoid":
        y = jax.nn.sigmoid(y)
    o_ref[...] = y


def _lora(x, w, bias, pre=None, post=None, tm=1024):
    n, r = x.shape
    n_out = w.shape[1]
    if bias is None:
        bias = jnp.zeros((n_out,), F32)
    return pl.pallas_call(
        functools.partial(_lora_kernel, pre=pre, post=post),
        grid=(n // tm,),
        in_specs=[pl.BlockSpec((tm, r), lambda i: (i, 0)),
                  pl.BlockSpec((r, n_out), lambda i: (0, 0)),
                  pl.BlockSpec((1, n_out), lambda i: (0, 0))],
        out_specs=pl.BlockSpec((tm, n_out), lambda i: (i, 0)),
        out_shape=jax.ShapeDtypeStruct((n, n_out), F32),
        compiler_params=pltpu.CompilerParams(dimension_semantics=("arbitrary",),
                                             vmem_limit_bytes=VMEM_LIMIT_BYTES),
        name="lora",
    )(x, w, bias.reshape(1, n_out))


def _heads(x, n):
    b, t, _ = x.shape
    return x.reshape(b, t, n, -1).transpose(0, 2, 1, 3)


def _merge_heads(x):
    b, h, t, d = x.shape
    return x.transpose(0, 2, 1, 3).reshape(b, t, h * d)


def _head_norm(x, eps, center):
    if center:
        x = x - jnp.mean(x, axis=-1, keepdims=True)
    return x * lax.rsqrt(jnp.mean(x * x, axis=-1, keepdims=True) + eps)


def _l2norm(x):
    return x * lax.rsqrt(jnp.sum(x * x, axis=-1, keepdims=True) + NORM_EPS)


def _grid_rotary(x):
    t, dk = x.shape[2], x.shape[3]
    rows = t // GRID_W
    row = jnp.broadcast_to(jnp.arange(rows, dtype=F32)[:, None], (rows, GRID_W)).reshape(t)
    col = jnp.broadcast_to(jnp.arange(GRID_W, dtype=F32)[None, :], (rows, GRID_W)).reshape(t)
    quarter = dk // 4
    inv = ROPE_BASE ** (-jnp.arange(quarter, dtype=F32) / quarter)
    ang = jnp.concatenate([row[:, None] * inv, col[:, None] * inv], axis=-1)
    cos, sin = jnp.cos(ang), jnp.sin(ang)
    x1, x2 = x[..., :dk // 2], x[..., dk // 2:]
    return jnp.concatenate([x1 * cos - x2 * sin, x1 * sin + x2 * cos], axis=-1)


def _retention_log_decay(exp0):
    h = jnp.arange(RET_HEADS, dtype=F32)
    return jnp.log1p(-jnp.power(2.0, -(exp0 + h)))


def _row_layout(x, tc=256):
    b, h, t = x.shape
    tc = min(tc, t)
    return x.reshape(b, h, t // tc, tc // CHUNK, CHUNK)


def _mixer_gla_ret(z, la_f, la_b, b, t, states, latent, gla_norm, ret_norm):
    z = z.reshape(b, t, -1)
    gq, gk, gv, gg, _, _, rq, rk, rv, rg = _split_sizes(z, L0_SIZES)
    s_gf, s_gb, s_rf, s_rb = states if states is not None else (None,) * 4
    q = _heads(gq, GLA_HEADS) * GLA_DK ** -0.5
    k = _heads(gk, GLA_HEADS)
    v = _heads(gv, GLA_HEADS)
    laf = _heads(la_f.reshape(b, t, -1), GLA_HEADS)
    lab = _heads(la_b.reshape(b, t, -1), GLA_HEADS)
    o, n_gf, n_gb = _bidir_scan("gla", (q, k, v, laf), (q, k, v, lab), s_gf, s_gb, GLA_DV)
    o_gla = _merge_heads(_head_norm(o, NORM_EPS, False) * gla_norm) * jax.nn.silu(gg)
    q = _heads(rq, RET_HEADS)
    k = _heads(rk, RET_HEADS)
    if latent:
        q, k = _grid_rotary(q), _grid_rotary(k)
    q = q * RET_DK ** -0.5
    v = _heads(rv, RET_HEADS)
    ones = jnp.ones((b, RET_HEADS, t), F32)
    lg_f = _row_layout(ones * _retention_log_decay(RET_DECAY_EXP_FWD)[None, :, None])
    lg_b = _row_layout(ones * _retention_log_decay(RET_DECAY_EXP_BWD)[None, :, None])
    dummy = _row_layout(ones)
    o, n_rf, n_rb = _bidir_scan("scalar", (q, k, v, lg_f, dummy), (q, k, v, lg_b, dummy), s_rf, s_rb, RET_DV)
    o_ret = _merge_heads(_head_norm(o, NORM_EPS, True) * ret_norm) * jax.nn.silu(rg)
    u = jnp.concatenate([o_gla, o_ret], axis=-1).reshape(b * t, -1)
    return u, (n_gf, n_gb, n_rf, n_rb)


def _centred_dwconv(x, w):
    t = x.shape[1]
    xp = jnp.pad(x, ((0, 0), (CONV_K // 2, CONV_K // 2), (0, 0)))
    out = xp[:, 0:t] * w[0]
    for j in range(1, CONV_K):
        out = out + xp[:, j:j + t] * w[j]
    return out


def _centred_shift_mix(z, mu):
    zp = jnp.pad(z, ((0, 0), (1, 1), (0, 0)))
    shifted = 0.5 * (zp[:, :-2] + zp[:, 2:])
    return z + mu * (shifted - z)


def _mixer_gdn_rwkv(z, b, t, states, conv_w, A_log_f, dt_bias_f, A_log_b, dt_bias_b, gdn_norm,
                    mu, w0_f, w2_f, a0_f, a2_f, w0_b, w2_b, a0_b, a2_b, g2, k_k, k_a, r_k, ln_w, ln_b):
    z = z.reshape(b, t, -1)
    z_gdn, z_rwkv = z[..., :GDN_IN], z[..., GDN_IN:GDN_IN + RWKV_IN]
    s_df, s_db, s_wf, s_wb = states if states is not None else (None,) * 4
    qkv = jax.nn.silu(_centred_dwconv(z_gdn[..., :GDN_QKV], conv_w))
    gq, gk, gv = _split_sizes(qkv, (GDN_QK, GDN_QK, GDN_V))
    gg, a_f, a_b, b_f, b_b = _split_sizes(z_gdn[..., GDN_QKV:], GDN_REST_SIZES)
    q = _l2norm(_heads(gq, GDN_HEADS)) * GDN_DK ** -0.5
    k = _l2norm(_heads(gk, GDN_HEADS))
    v = _heads(gv, GDN_HEADS)
    sw = lambda x: jnp.swapaxes(x, 1, 2)
    la_f = _row_layout(-jnp.exp(A_log_f)[:, None] * jax.nn.softplus(sw(a_f) + dt_bias_f[:, None]))
    la_b = _row_layout(-jnp.exp(A_log_b)[:, None] * jax.nn.softplus(sw(a_b) + dt_bias_b[:, None]))
    beta_f = _row_layout(jax.nn.sigmoid(sw(b_f)))
    beta_b = _row_layout(jax.nn.sigmoid(sw(b_b)))
    o, n_df, n_db = _bidir_scan("scalar", (q, k, v, la_f, beta_f), (q, k, v, la_b, beta_b), s_df, s_db,
                                GDN_DV, delta=True)
    o_gdn = _merge_heads(_head_norm(o, NORM_EPS, False) * gdn_norm) * jax.nn.silu(gg)
    zr = _centred_shift_mix(z_rwkv, mu)
    r, kr, vr, wd_f, wd_b, ad_f, ad_b, gd = _split_sizes(zr, RWKV_SIZES)
    rh = _heads(r, RWKV_HEADS)
    vh = _heads(vr, RWKV_HEADS)
    kk = _l2norm(_heads(kr * k_k, RWKV_HEADS))
    flat = lambda x: x.reshape(b * t, -1)

    def direction(w0, wd, w2, a0, ad, a2):
        lw = _lora(flat(wd), w2, w0, pre="tanh", post="rwkv_w").reshape(b, t, -1)
        a = _lora(flat(ad), a2, a0, post="sigmoid").reshape(b, t, -1)
        kd = _heads(kr * (1.0 + (a - 1.0) * k_a), RWKV_HEADS)
        bonus = jnp.sum(rh * kd * r_k[:, None, :], axis=-1, keepdims=True) * vh
        return (rh, _heads(lw, RWKV_HEADS), kd, vh, -kk, kk * _heads(a, RWKV_HEADS)), bonus

    args_f, bonus_f = direction(w0_f, wd_f, w2_f, a0_f, ad_f, a2_f)
    args_b, bonus_b = direction(w0_b, wd_b, w2_b, a0_b, ad_b, a2_b)
    o, n_wf, n_wb = _bidir_scan("rwkv", args_f, args_b, s_wf, s_wb, RWKV_N)
    y = _merge_heads(_head_norm(o, RWKV_GN_EPS, True)) * ln_w + ln_b + _merge_heads(bonus_f + bonus_b)
    gate = _lora(flat(gd), g2, None, pre="sigmoid").reshape(b, t, -1)
    y = y * gate
    u = jnp.concatenate([o_gdn, y], axis=-1).reshape(b * t, -1)
    return u, (n_df, n_db, n_wf, n_wb)


def _pad_cols(w, mult):
    n = w.shape[1]
    pad = (-n) % mult
    return jnp.pad(w, ((0, 0), (0, pad))) if pad else w


def kernel(x_prompt, x_sample, c, c_ctx, state_l0_gla_fwd, state_l0_gla_bwd, state_l0_ret_fwd, state_l0_ret_bwd, state_l1_gdn_fwd, state_l1_gdn_bwd, state_l1_rwkv_fwd, state_l1_rwkv_bwd, l0_w_mod, l0_b_mod, l0_norm1, l0_norm2, l0_norm3, l0_ffn1_wg, l0_ffn1_wu, l0_ffn1_wd, l0_ffn2_wg, l0_ffn2_wu, l0_ffn2_wd, l0_w_in, l0_w_out, l0_gla_gk_up_fwd, l0_gla_gk_b_fwd, l0_gla_gk_up_bwd, l0_gla_gk_b_bwd, l0_gla_norm, l0_ret_norm, l1_w_mod, l1_b_mod, l1_norm1, l1_norm2, l1_norm3, l1_ffn1_wg, l1_ffn1_wu, l1_ffn1_wd, l1_ffn2_wg, l1_ffn2_wu, l1_ffn2_wd, l1_w_in, l1_w_out, l1_gdn_conv, l1_gdn_A_log_fwd, l1_gdn_dt_bias_fwd, l1_gdn_A_log_bwd, l1_gdn_dt_bias_bwd, l1_gdn_norm, l1_rwkv_mu, l1_rwkv_w0_fwd, l1_rwkv_w2_fwd, l1_rwkv_a0_fwd, l1_rwkv_a2_fwd, l1_rwkv_w0_bwd, l1_rwkv_w2_bwd, l1_rwkv_a0_bwd, l1_rwkv_a2_bwd, l1_rwkv_g2, l1_rwkv_k_k, l1_rwkv_k_a, l1_rwkv_r_k, l1_rwkv_ln_w, l1_rwkv_ln_b, final_norm):
    bp, tp, d = x_prompt.shape
    bs, ts, _ = x_sample.shape
    n_prompt = bp * tp
    x = jnp.concatenate([x_prompt.reshape(n_prompt, d), x_sample.reshape(bs * ts, d)], axis=0)
    cond = jnp.concatenate([c_ctx[None, :], c, jnp.zeros((8 - 1 - bs, d), F32)], axis=0)
    common = (
        (l0_w_mod, l0_b_mod, (l0_norm1, l0_norm2, l0_norm3), (l0_ffn1_wg, l0_ffn1_wu, l0_ffn1_wd),
         (l0_ffn2_wg, l0_ffn2_wu, l0_ffn2_wd), l0_w_in, l0_w_out),
        (l1_w_mod, l1_b_mod, (l1_norm1, l1_norm2, l1_norm3), (l1_ffn1_wg, l1_ffn1_wu, l1_ffn1_wd),
         (l1_ffn2_wg, l1_ffn2_wu, l1_ffn2_wd), l1_w_in, l1_w_out),
    )
    caches = (
        (state_l0_gla_fwd, state_l0_gla_bwd, state_l0_ret_fwd, state_l0_ret_bwd),
        (state_l1_gdn_fwd, state_l1_gdn_bwd, state_l1_rwkv_fwd, state_l1_rwkv_bwd),
    )
    new_states = []
    for layer in range(2):
        w_mod, b_mod, norms, ffn1, ffn2, w_in, w_out = common[layer]
        m = _adaln(cond, w_mod, b_mod)
        mods = [t_.reshape(8, 1, d) for t_ in jnp.split(m, N_MOD, axis=-1)]
        sh1, sc1, g1, sh2, sc2, g2, sh3, sc3, g3 = mods
        bf = lambda w: w.astype(BF16)
        x = _ffn(x, (sh1, sc1, g1), norms[0], bf(ffn1[0]), bf(ffn1[1]), bf(ffn1[2]), final_norm,
                 n_prompt, ts, False)
        z = _proj_in(x, sh2, sc2, norms[1], bf(_pad_cols(w_in, 512)), n_prompt, ts)
        zp, zs = z[:n_prompt], z[n_prompt:]
        if layer == 0:
            gdf = z[:, 2 * GLA_QK + 2 * GLA_V: 2 * GLA_QK + 2 * GLA_V + GLA_LOWRANK]
            gdb = z[:, 2 * GLA_QK + 2 * GLA_V + GLA_LOWRANK: 2 * GLA_QK + 2 * GLA_V + 2 * GLA_LOWRANK]
            la_f = _lora(gdf, l0_gla_gk_up_fwd, l0_gla_gk_b_fwd, post="logsig_gla")
            la_b = _lora(gdb, l0_gla_gk_up_bwd, l0_gla_gk_b_bwd, post="logsig_gla")
            n_l0 = sum(L0_SIZES)
            up, st = _mixer_gla_ret(zp[:, :n_l0], la_f[:n_prompt], la_b[:n_prompt], bp, tp, None, False,
                                    l0_gla_norm, l0_ret_norm)
            us, _ = _mixer_gla_ret(zs[:, :n_l0], la_f[n_prompt:], la_b[n_prompt:], bs, ts, caches[0], True,
                                   l0_gla_norm, l0_ret_norm)
        else:
            mp = (l1_gdn_conv, l1_gdn_A_log_fwd, l1_gdn_dt_bias_fwd, l1_gdn_A_log_bwd, l1_gdn_dt_bias_bwd,
                  l1_gdn_norm, l1_rwkv_mu, l1_rwkv_w0_fwd, l1_rwkv_w2_fwd, l1_rwkv_a0_fwd, l1_rwkv_a2_fwd,
                  l1_rwkv_w0_bwd, l1_rwkv_w2_bwd, l1_rwkv_a0_bwd, l1_rwkv_a2_bwd,
                  l1_rwkv_g2, l1_rwkv_k_k, l1_rwkv_k_a, l1_rwkv_r_k, l1_rwkv_ln_w, l1_rwkv_ln_b)
            up, st = _mixer_gdn_rwkv(zp, bp, tp, None, *mp)
            us, _ = _mixer_gdn_rwkv(zs, bs, ts, caches[1], *mp)
        new_states.extend(st)
        u = jnp.concatenate([up, us], axis=0)
        x = _proj_out(x, u, g2, bf(w_out), n_prompt, ts)
        x = _ffn(x, (sh3, sc3, g3), norms[2], bf(ffn2[0]), bf(ffn2[1]), bf(ffn2[2]), final_norm,
                 n_prompt, ts, layer == 1)
    y_prompt = x[:n_prompt].reshape(bp, tp, d)
    y_sample = x[n_prompt:].reshape(bs, ts, d)
    return (y_prompt, y_sample, *new_states)
```

```python
import functools

import numpy as np
import jax
import jax.numpy as jnp
from jax import lax
from jax.experimental import pallas as pl
from jax.experimental.pallas import tpu as pltpu

F32 = jnp.float32
BF16 = jnp.bfloat16

D_MODEL = 2048
D_FF = 5632
N_MOD = 9
NORM_EPS = 1e-6
GRID_W = 64
ROPE_BASE = 10000.0
CHUNK = 64
SUB = 16
TIME_TILE = 256

GLA_HEADS, GLA_DK, GLA_DV, GLA_LOWRANK, GLA_GATE_NORM = 4, 128, 256, 16, 16.0
GLA_QK, GLA_V = GLA_HEADS * GLA_DK, GLA_HEADS * GLA_DV
RET_HEADS, RET_DK, RET_DV = 4, 128, 256
RET_QK, RET_V = RET_HEADS * RET_DK, RET_HEADS * RET_DV
RET_DECAY_EXP_FWD, RET_DECAY_EXP_BWD = 5.0, 5.5
GDN_HEADS, GDN_DK, GDN_DV = 8, 128, 128
GDN_QK, GDN_V = GDN_HEADS * GDN_DK, GDN_HEADS * GDN_DV
GDN_QKV = 2 * GDN_QK + GDN_V
CONV_K = 5
GDN_IN = GDN_QKV + GDN_V + 4 * GDN_HEADS
RWKV_HEADS, RWKV_N = 16, 64
RWKV_C = RWKV_HEADS * RWKV_N
RWKV_DECAY_LORA, RWKV_AAA_LORA, RWKV_GATE_LORA = 64, 64, 128
RWKV_GN_EPS = 64e-5
RWKV_IN = 3 * RWKV_C + 2 * RWKV_DECAY_LORA + 2 * RWKV_AAA_LORA + RWKV_GATE_LORA

L0_GQ, L0_GK, L0_GV, L0_GG = 0, GLA_QK, 2 * GLA_QK, 2 * GLA_QK + GLA_V
L0_RQ = L0_GG + GLA_V
L0_RK, L0_RV = L0_RQ + RET_QK, L0_RQ + 2 * RET_QK
L0_RG = L0_RV + RET_V
L0_GDF = L0_RG + RET_V
L0_GDB = L0_GDF + GLA_LOWRANK
L0_END = L0_GDB + GLA_LOWRANK
L1_GG = GDN_QKV
L1_RWKV = GDN_QKV + GDN_V
L1_SCAL = L1_RWKV + RWKV_IN
L1_END = L1_SCAL + 4 * GDN_HEADS
PROJ_TN = 512

VMEM_LIMIT_BYTES = 56 * 1024 * 1024


def _cparams(n_axes):
    return pltpu.CompilerParams(dimension_semantics=("arbitrary",) * n_axes, vmem_limit_bytes=VMEM_LIMIT_BYTES)


def _bdot(a, b):
    return jnp.dot(a.astype(BF16), b.astype(BF16), preferred_element_type=F32)


def _bdot_nt(a, b):
    return lax.dot_general(a.astype(BF16), b.astype(BF16), (((1,), (1,)), ((), ())),
                           preferred_element_type=F32)


def _bdot_tn(a, b):
    return lax.dot_general(a.astype(BF16), b.astype(BF16), (((0,), (0,)), ((), ())),
                           preferred_element_type=F32)


def _bmm(a, b):
    return lax.dot_general(a.astype(BF16), b.astype(BF16), (((2,), (1,)), ((0,), (0,))),
                           preferred_element_type=F32)


def _bmm_nt(a, b):
    return lax.dot_general(a.astype(BF16), b.astype(BF16), (((2,), (2,)), ((0,), (0,))),
                           preferred_element_type=F32)


def _bmm_tn(a, b):
    return lax.dot_general(a.astype(BF16), b.astype(BF16), (((1,), (1,)), ((0,), (0,))),
                           preferred_element_type=F32)


def _split2(x):
    hi = x.astype(BF16)
    lo = (x - hi.astype(F32)).astype(BF16)
    return hi, lo


def _bmm3(a, b):
    ah, al = _split2(a)
    bh, bl = _split2(b)
    d = lambda x, y: lax.dot_general(x, y, (((2,), (1,)), ((0,), (0,))), preferred_element_type=F32)
    return d(ah, bh) + (d(ah, bl) + d(al, bh))


def _split3(x):
    x1 = x.astype(BF16)
    r1 = x - x1.astype(F32)
    x2 = r1.astype(BF16)
    x3 = (r1 - x2.astype(F32)).astype(BF16)
    return x1, x2, x3


def _cumsum_rows(tri, x):
    x1, x2, x3 = _split3(x)
    d = lambda y: jnp.dot(tri, y, preferred_element_type=F32)
    return d(x1) + (d(x2) + d(x3))


def _bcumsum(tri_b, x):
    x1, x2, x3 = _split3(x)
    d = lambda y: lax.dot_general(tri_b, y, (((2,), (1,)), ((0,), (0,))), preferred_element_type=F32)
    return d(x1) + (d(x2) + d(x3))


def _iota2(n, m, axis):
    return lax.broadcasted_iota(jnp.int32, (n, m), axis)


def _row_to_col(row_vec, eye):
    return jnp.sum(jnp.where(eye, row_vec, 0.0), axis=1, keepdims=True)


def _chunk_masks(c, rev):
    row = _iota2(c, c, 0)
    col = _iota2(c, c, 1)
    eye = row == col
    incl = (row >= col) if not rev else (row <= col)
    strict = (row > col) if not rev else (row < col)
    same_blk = jnp.right_shift(row, 4) == jnp.right_shift(col, 4)
    return eye, incl, strict, same_blk


def _l2norm(x):
    return x * lax.rsqrt(jnp.sum(x * x, axis=-1, keepdims=True) + NORM_EPS)


def _silu(x):
    return x * jax.nn.sigmoid(x)


def _softplus(x):
    return jnp.maximum(x, 0.0) + jnp.log1p(jnp.exp(-jnp.abs(x)))


def _split_heads(x, hb, ncs):
    d = x.shape[1] // hb
    return jnp.concatenate([x[:, h * d:(h + 1) * d].reshape(ncs, CHUNK, d) for h in range(hb)], axis=0)


def _merge_heads(y, hb, ncs):
    d = y.shape[-1]
    return jnp.concatenate([y[h * ncs:(h + 1) * ncs].reshape(ncs * CHUNK, d) for h in range(hb)], axis=1)


def _gla_chunk(q, k, v, lc, s, rev):
    c, kd = q.shape
    row = _iota2(c, c, 0)
    col = _iota2(c, c, 1)
    incl = (row >= col) if not rev else (row <= col)
    tri = jnp.where(incl, 1.0, 0.0).astype(BF16)
    g = _cumsum_rows(tri, lc)
    ns = c // SUB
    rows = []
    for i in range(ns):
        first = (i == 0) if not rev else (i == ns - 1)
        if first:
            rows.append(jnp.zeros((SUB, c), F32))
            continue
        r0 = i * SUB
        gref = g[r0 - 1:r0, :] if not rev else g[r0 + SUB:r0 + SUB + 1, :]
        qs = q[r0:r0 + SUB, :] * jnp.exp(g[r0:r0 + SUB, :] - gref)
        ks = k * jnp.exp(jnp.minimum(gref - g, 0.0))
        rows.append(_bdot_nt(qs, ks))
    att = jnp.concatenate(rows, axis=0)
    blk_r = jnp.right_shift(row, 4)
    blk_c = jnp.right_shift(col, 4)
    att = jnp.where((blk_r > blk_c) if not rev else (blk_r < blk_c), att, 0.0)
    dist = (row - col) if not rev else (col - row)
    dist = jnp.where(blk_r == blk_c, dist, -1)
    for d in range(SUB):
        if d == 0:
            ksh, gsh = k, g
        else:
            sh = d if not rev else c - d
            ksh = pltpu.roll(k, sh, 0)
            gsh = pltpu.roll(g, sh, 0)
        e = jnp.exp(jnp.minimum(g - gsh, 0.0))
        band = jnp.sum(q * ksh * e, axis=1, keepdims=True)
        att = jnp.where(dist == d, band, att)
    o = _bdot(att, v) + _bdot(q * jnp.exp(g), s)
    g_end = g[c - 1:c, :] if not rev else g[0:1, :]
    eye_k = _iota2(kd, kd, 0) == _iota2(kd, kd, 1)
    dcol = _row_to_col(jnp.exp(g_end), eye_k)
    s_new = dcol * s + _bdot_tn(k * jnp.exp(g_end - g), v)
    return o, s_new


def _gla_kernel(*refs, rev, has_s0, hb, ncs):
    q_ref, k_ref, v_ref, la_ref = refs[:4]
    pos = 4
    s0_ref = None
    if has_s0:
        s0_ref = refs[pos]
        pos += 1
    o_ref, sout_ref, s_scr = refs[pos], refs[pos + 1], refs[pos + 2]
    t = pl.program_id(2)
    nt = pl.num_programs(2)

    @pl.when(t == 0)
    def _():
        if has_s0:
            s_scr[...] = s0_ref[0]
        else:
            s_scr[...] = jnp.zeros(s_scr.shape, F32)

    def body(ci, carry):
        cc = (ncs - 1 - ci) if rev else ci
        rows = pl.ds(pl.multiple_of(cc * CHUNK, CHUNK), CHUNK)
        loaded = []
        for h in range(hb):
            kcols = slice(h * GLA_DK, (h + 1) * GLA_DK)
            vcols = slice(h * GLA_DV, (h + 1) * GLA_DV)
            loaded.append((q_ref[rows, kcols] * GLA_DK ** -0.5, k_ref[rows, kcols], v_ref[rows, vcols],
                           la_ref[rows, kcols], s_scr[h]))
        results = [_gla_chunk(*x, rev) for x in loaded]
        o_ref[rows, :] = jnp.concatenate([o for o, _ in results], axis=1)
        for h, (_, s_new) in enumerate(results):
            s_scr[h] = s_new
        return carry

    lax.fori_loop(0, ncs, body, 0)

    @pl.when(t == nt - 1)
    def _():
        sout_ref[0] = s_scr[...]


def _tri_inverse_b(n_mat, eye, same_blk):
    eyef = jnp.where(eye, 1.0, 0.0)
    nd = jnp.where(same_blk, n_mat, 0.0)
    no = n_mat - nd
    p = eyef - nd
    m = nd
    for _ in range(3):
        m = _bmm(m, m)
        p = p + _bmm(p, m)
    mm = _bmm(p, no)
    q = eyef - mm
    q = q + _bmm(q, _bmm(mm, mm))
    x0 = _bmm(q, p)
    e = eyef - x0 - _bmm3(n_mat, x0)
    return x0 + _bmm(x0, e)


def _rwkv_phase1(r, lw, k, v, a, b, rev):
    nb, c, n = r.shape
    eye, incl, strict, same_blk = _chunk_masks(c, rev)
    tri_b = jnp.broadcast_to(jnp.where(incl, 1.0, 0.0).astype(BF16)[None], (nb, c, c))
    g = _bcumsum(tri_b, lw)
    eg = jnp.exp(g)
    einv = jnp.exp(-g)
    at = a * jnp.exp(g - lw)
    rt = r * eg
    ar = jnp.concatenate([at, rt], axis=1)
    ab = _bmm_nt(ar, b * einv)
    ak = _bmm_nt(ar, k * einv)
    a_ab = jnp.where(strict, ab[:, :c], 0.0)
    a_rb = jnp.where(incl, ab[:, c:], 0.0)
    a_ak = jnp.where(strict, ak[:, :c], 0.0)
    a_rk = jnp.where(incl, ak[:, c:], 0.0)
    t_inv = _tri_inverse_b(-a_ab, eye, same_blk)
    w = _bmm(t_inv, at)
    u0 = _bmm(t_inv, _bmm(a_ak, v))
    r2 = rt + _bmm(a_rb, w)
    y0 = _bmm(a_rk, v) + _bmm(a_rb, u0)
    g_end = g[:, c - 1:c, :] if not rev else g[:, 0:1, :]
    dec = jnp.exp(g_end - g)
    bd = b * dec
    m = _bmm_tn(bd, w)
    s_add = _bmm_tn(bd, u0) + _bmm_tn(k * dec, v)
    eye_n = _iota2(n, n, 0) == _iota2(n, n, 1)
    dcol = jnp.sum(jnp.where(eye_n, jnp.exp(g_end), 0.0), axis=2, keepdims=True)
    return r2, y0, m, s_add, dcol


def _scalar_phase1(q, k, v, lc_row, beta_row, rev, delta):
    nb, c, _ = q.shape
    eye, incl, strict, same_blk = _chunk_masks(c, rev)
    row = _iota2(c, c, 0)
    col = _iota2(c, c, 1)
    before = (row <= col) if not rev else (row >= col)
    to_col = lambda x: jnp.sum(jnp.where(eye, x, 0.0), axis=2, keepdims=True)
    lc_col = to_col(lc_row)
    g_row = jnp.sum(jnp.where(before, lc_col, 0.0), axis=1, keepdims=True)
    g_col = to_col(g_row)
    rel = jnp.where(incl, jnp.exp(jnp.minimum(g_col - g_row, 0.0)), 0.0)
    eg = jnp.exp(g_col)
    g_end = g_col[:, c - 1:c, :] if not rev else g_col[:, 0:1, :]
    kdec = k * jnp.exp(g_end - g_col)
    if delta:
        beta_col = to_col(beta_row)
        qkk = _bmm_nt(jnp.concatenate([q, k], axis=1), k)
        p = qkk[:, :c] * rel
        n_mat = jnp.where(strict, beta_col * rel * qkk[:, c:], 0.0)
        t_inv = _tri_inverse_b(n_mat, eye, same_blk)
        sol_v = _bmm(t_inv, beta_col * v)
        sol_k = _bmm(t_inv, (beta_col * eg) * k)
        r2 = q * eg - _bmm(p, sol_k)
        y0 = _bmm(p, sol_v)
        m = _bmm_tn(kdec, sol_k)
        s_add = _bmm_tn(kdec, sol_v)
    else:
        p = _bmm_nt(q, k) * rel
        r2 = q * eg
        y0 = _bmm(p, v)
        m = None
        s_add = _bmm_tn(kdec, v)
    return r2, y0, m, s_add, jnp.exp(g_end)


def _phase2(r2, y0, m, m_sign, s_add, dec, s, o_ref, hb, ncs, rev):
    unb = lambda x: x.reshape((hb, ncs) + x.shape[1:])
    r2, y0, s_add, dec = unb(r2), unb(y0), unb(s_add), unb(dec)
    if m is not None:
        m = unb(m)
    for ci in range(ncs):
        cc = (ncs - 1 - ci) if rev else ci
        y = y0[:, cc] + _bmm(r2[:, cc], s)
        o_ref[cc * CHUNK:(cc + 1) * CHUNK, :] = jnp.concatenate([y[h] for h in range(hb)], axis=1)
        s_new = dec[:, cc] * s + s_add[:, cc]
        if m is not None:
            s_new = s_new + m_sign * _bmm(m[:, cc], s)
        s = s_new
    return s


def _init_state(s_scr, s0_ref):
    @pl.when(pl.program_id(2) == 0)
    def _():
        if s0_ref is not None:
            s_scr[...] = s0_ref[0]
        else:
            s_scr[...] = jnp.zeros(s_scr.shape, F32)


def _scalar_kernel(*refs, rev, has_s0, hb, ncs, delta, rotary):
    q_ref, k_ref, v_ref, la_ref, be_ref = refs[:5]
    pos = 5
    cos_ref = sin_ref = s0_ref = None
    if rotary:
        cos_ref, sin_ref = refs[pos], refs[pos + 1]
        pos += 2
    if has_s0:
        s0_ref = refs[pos]
        pos += 1
    o_ref, sout_ref, s_scr = refs[pos], refs[pos + 1], refs[pos + 2]
    _init_state(s_scr, s0_ref)
    nb = hb * ncs
    q = q_ref[...]
    k = k_ref[...]
    dk = q.shape[1] // hb
    if rotary:
        cos2, sin2 = cos_ref[...], sin_ref[...]
        rot = lambda x: jnp.concatenate(
            [x[:, h * dk:(h + 1) * dk] * cos2 + pltpu.roll(x[:, h * dk:(h + 1) * dk], dk // 2, 1) * sin2
             for h in range(hb)], axis=1)
        q, k = rot(q), rot(k)
    q = _split_heads(q, hb, ncs)
    k = _split_heads(k, hb, ncs)
    if delta:
        q, k = _l2norm(q), _l2norm(k)
    q = q * dk ** -0.5
    v = _split_heads(v_ref[...], hb, ncs)
    r2, y0, m, s_add, dec = _scalar_phase1(q, k, v, la_ref[0].reshape(nb, 1, CHUNK),
                                           be_ref[0].reshape(nb, 1, CHUNK), rev, delta)
    s = _phase2(r2, y0, m, -1.0, s_add, dec, s_scr[...], o_ref, hb, ncs, rev)
    s_scr[...] = s

    @pl.when(pl.program_id(2) == pl.num_programs(2) - 1)
    def _():
        sout_ref[0] = s


def _rwkv_kernel(*refs, rev, has_s0, hb, ncs):
    r_ref, kr_ref, v_ref, lw_ref, a_ref, kk_ref, ka_ref, rk_ref = refs[:8]
    pos = 8
    s0_ref = None
    if has_s0:
        s0_ref = refs[pos]
        pos += 1
    o_ref, bonus_ref, sout_ref, s_scr = refs[pos], refs[pos + 1], refs[pos + 2], refs[pos + 3]
    _init_state(s_scr, s0_ref)
    r = r_ref[...]
    kr = kr_ref[...]
    a = a_ref[...]
    kd = kr * (1.0 + (a - 1.0) * ka_ref[...])
    sp = lambda x: _split_heads(x, hb, ncs)
    kk = _l2norm(sp(kr * kk_ref[...]))
    a_s = sp(a)
    v_s = sp(v_ref[...])
    bonus = jnp.sum(sp(r * kd * rk_ref[...]), axis=-1, keepdims=True) * v_s
    bonus_ref[...] = _merge_heads(bonus, hb, ncs)
    r2, y0, m, s_add, dec = _rwkv_phase1(sp(r), sp(lw_ref[...]), sp(kd), v_s, -kk, kk * a_s, rev)
    s = _phase2(r2, y0, m, 1.0, s_add, dec, s_scr[...], o_ref, hb, ncs, rev)
    s_scr[...] = s

    @pl.when(pl.program_id(2) == pl.num_programs(2) - 1)
    def _():
        sout_ref[0] = s


def _scan_call(kind, tok_ins, extra_ins, s0, rev, b, t, heads, dk, dv, n_tok_out=1, hb=4, **kw):
    tc = min(TIME_TILE, t)
    nt = t // tc
    ncs = tc // CHUNK
    tile = lambda ti: (nt - 1 - ti) if rev else ti

    def tok_spec(width, off):
        bw = hb * width
        assert off % bw == 0
        cb = off // bw
        return pl.BlockSpec((tc, bw), lambda bi, hi, ti: (bi * nt + tile(ti), cb + hi))

    in_specs = [tok_spec(w, off) for _, w, off in tok_ins] + [mk(tile) for _, mk in extra_ins]
    args = [x for x, _, _ in tok_ins] + [x for x, _ in extra_ins]
    state_spec = pl.BlockSpec((1, hb, dk, dv), lambda bi, hi, ti: (bi, hi, 0, 0))
    if s0 is not None:
        in_specs.append(state_spec)
        args.append(s0)
    body = {"gla": _gla_kernel, "scalar": _scalar_kernel, "rwkv": _rwkv_kernel}[kind]
    outs = pl.pallas_call(
        functools.partial(body, rev=rev, has_s0=s0 is not None, hb=hb, ncs=ncs, **kw),
        grid=(b, heads // hb, nt),
        in_specs=in_specs,
        out_specs=[tok_spec(dv, 0)] * n_tok_out + [state_spec],
        out_shape=[jax.ShapeDtypeStruct((b * t, heads * dv), F32)] * n_tok_out
        + [jax.ShapeDtypeStruct((b, heads, dk, dv), F32)],
        scratch_shapes=[pltpu.VMEM((hb, dk, dv), F32)],
        compiler_params=_cparams(3),
        name=f"scan_{kind}_{'bwd' if rev else 'fwd'}",
    )(*args)
    return outs


def _row_spec(hb, ncs, nt):
    return lambda tile: pl.BlockSpec((1, hb, ncs, 1, CHUNK), lambda bi, hi, ti: (bi, hi, tile(ti), 0, 0))


def _row_layout(x):
    b, h, t = x.shape
    return x.reshape(b, h, t // CHUNK, 1, CHUNK)


def _modulated_norm(x, nw, shift, scale):
    y = x * lax.rsqrt(jnp.mean(x * x, axis=-1, keepdims=True) + NORM_EPS) * nw
    return y * (1.0 + scale) + shift


def _adaln_kernel(c_ref, w_ref, b_ref, o_ref):
    o_ref[...] = _bdot(_silu(c_ref[...]), w_ref[...]) + b_ref[...]


def _adaln(cond, w_mod, b_mod):
    r, d = cond.shape
    n = w_mod.shape[1]
    tn = 1024
    return pl.pallas_call(
        _adaln_kernel,
        grid=(n // tn,),
        in_specs=[pl.BlockSpec((r, d), lambda j: (0, 0)),
                  pl.BlockSpec((d, tn), lambda j: (0, j)),
                  pl.BlockSpec((1, tn), lambda j: (0, j))],
        out_specs=pl.BlockSpec((r, tn), lambda j: (0, j)),
        out_shape=jax.ShapeDtypeStruct((r, n), F32),
        compiler_params=_cparams(1),
        name="adaln",
    )(cond, w_mod, b_mod.reshape(1, n))


def _ffn_kernel(x_ref, sh_ref, sc_ref, g_ref, nw_ref, wg_ref, wu_ref, wd_ref, fn_ref, o_ref, h_scr, *,
                nf, final_norm):
    f = pl.program_id(1)

    @pl.when(f == 0)
    def _():
        h = _modulated_norm(x_ref[...], nw_ref[...], sh_ref[0], sc_ref[0])
        h_scr[...] = h.astype(BF16)
        o_ref[...] = jnp.zeros(o_ref.shape, F32)

    h = h_scr[...]
    gate = jnp.dot(h, wg_ref[...], preferred_element_type=F32)
    up = jnp.dot(h, wu_ref[...], preferred_element_type=F32)
    o_ref[...] += jnp.dot((_silu(gate) * up).astype(BF16), wd_ref[...], preferred_element_type=F32)

    @pl.when(f == nf - 1)
    def _():
        y = x_ref[...] + (0.5 * g_ref[0]) * o_ref[...]
        if final_norm:
            y = y * lax.rsqrt(jnp.mean(y * y, axis=-1, keepdims=True) + NORM_EPS) * fn_ref[...]
        o_ref[...] = y


def _ffn(x, mods, tokens_per_mod, nw, wg, wu, wd, fnorm, final_norm, tm=512, tf=512):
    n, d = x.shape
    shift, scale, gate = mods
    nf = wg.shape[1] // tf
    mod_spec = pl.BlockSpec((1, 1, d), lambda i, f: ((i * tm) // tokens_per_mod, 0, 0))
    vec_spec = pl.BlockSpec((1, d), lambda i, f: (0, 0))
    return pl.pallas_call(
        functools.partial(_ffn_kernel, nf=nf, final_norm=final_norm),
        grid=(n // tm, nf),
        in_specs=[pl.BlockSpec((tm, d), lambda i, f: (i, 0)), mod_spec, mod_spec, mod_spec, vec_spec,
                  pl.BlockSpec((d, tf), lambda i, f: (0, f)),
                  pl.BlockSpec((d, tf), lambda i, f: (0, f)),
                  pl.BlockSpec((tf, d), lambda i, f: (f, 0)),
                  vec_spec],
        out_specs=pl.BlockSpec((tm, d), lambda i, f: (i, 0)),
        out_shape=jax.ShapeDtypeStruct((n, d), F32),
        scratch_shapes=[pltpu.VMEM((tm, d), BF16)],
        compiler_params=_cparams(2),
        name="ffn",
    )(x, shift, scale, gate, nw.reshape(1, d), wg, wu, wd, fnorm.reshape(1, d))


def _proj_in_kernel(x_ref, sh_ref, sc_ref, nw_ref, w_ref, o_ref, h_scr):
    @pl.when(pl.program_id(1) == 0)
    def _():
        h_scr[...] = _modulated_norm(x_ref[...], nw_ref[...], sh_ref[0], sc_ref[0]).astype(BF16)

    o_ref[...] = jnp.dot(h_scr[...], w_ref[...], preferred_element_type=F32)


def _proj_in(x, shift, scale, tokens_per_mod, nw, w, tm=512, tn=PROJ_TN):
    n, d = x.shape
    n_out = w.shape[1]
    mod_spec = pl.BlockSpec((1, 1, d), lambda i, j: ((i * tm) // tokens_per_mod, 0, 0))
    return pl.pallas_call(
        _proj_in_kernel,
        grid=(n // tm, n_out // tn),
        in_specs=[pl.BlockSpec((tm, d), lambda i, j: (i, 0)), mod_spec, mod_spec,
                  pl.BlockSpec((1, d), lambda i, j: (0, 0)),
                  pl.BlockSpec((d, tn), lambda i, j: (0, j))],
        out_specs=pl.BlockSpec((tm, tn), lambda i, j: (i, j)),
        out_shape=jax.ShapeDtypeStruct((n, n_out), F32),
        scratch_shapes=[pltpu.VMEM((tm, d), BF16)],
        compiler_params=_cparams(2),
        name="proj_in",
    )(x, shift, scale, nw.reshape(1, d), w)


def _head_norm_lanes(x, nheads, eps, center):
    d = x.shape[1] // nheads
    outs = []
    for h in range(nheads):
        xh = x[:, h * d:(h + 1) * d]
        if center:
            xh = xh - jnp.mean(xh, axis=-1, keepdims=True)
        outs.append(xh * lax.rsqrt(jnp.mean(xh * xh, axis=-1, keepdims=True) + eps))
    return jnp.concatenate(outs, axis=1)


def _head_norm_half_lanes(x, eps):
    lo = _iota2(1, 128, 1) < 64
    outs = []
    for j in range(x.shape[1] // 128):
        xb = x[:, j * 128:(j + 1) * 128]
        half = lambda y: jnp.where(lo, jnp.sum(jnp.where(lo, y, 0.0), axis=-1, keepdims=True),
                                   jnp.sum(jnp.where(lo, 0.0, y), axis=-1, keepdims=True)) * (1.0 / 64.0)
        xc = xb - half(xb)
        outs.append(xc * lax.rsqrt(half(xc * xc) + eps))
    return jnp.concatenate(outs, axis=1)


def _mix_out_l0_kernel(x_ref, gf_ref, gb_ref, rf_ref, rb_ref, gg_ref, rg_ref, g_ref, gn_ref, rn_ref, w_ref, o_ref):
    og = _head_norm_lanes(gf_ref[...] + gb_ref[...], GLA_HEADS, NORM_EPS, False) * gn_ref[...] * _silu(gg_ref[...])
    orr = _head_norm_lanes(rf_ref[...] + rb_ref[...], RET_HEADS, NORM_EPS, True) * rn_ref[...] * _silu(rg_ref[...])
    u = jnp.concatenate([og, orr], axis=1).astype(BF16)
    o_ref[...] = x_ref[...] + g_ref[0] * jnp.dot(u, w_ref[...], preferred_element_type=F32)


def _mix_out_l1_kernel(x_ref, df_ref, db_ref, wf_ref, wb_ref, bf_ref, bb_ref, gg_ref, gate_ref, g_ref,
                       dn_ref, lnw_ref, lnb_ref, w_ref, o_ref):
    od = _head_norm_lanes(df_ref[...] + db_ref[...], GDN_HEADS, NORM_EPS, False) * dn_ref[...] * _silu(gg_ref[...])
    y = _head_norm_half_lanes(wf_ref[...] + wb_ref[...], RWKV_GN_EPS) * lnw_ref[...] + lnb_ref[...]
    y = (y + bf_ref[...] + bb_ref[...]) * gate_ref[...]
    u = jnp.concatenate([od, y], axis=1).astype(BF16)
    o_ref[...] = x_ref[...] + g_ref[0] * jnp.dot(u, w_ref[...], preferred_element_type=F32)


def _mix_out(kern, x, tok_ins, gate_mod, tokens_per_mod, vecs, w, tm=256):
    n, d = x.shape
    in_specs = [pl.BlockSpec((tm, d), lambda i: (i, 0))]
    in_specs += [pl.BlockSpec((tm, bw), functools.partial(lambda i, cb: (i, cb), cb=cb)) for _, bw, cb in tok_ins]
    in_specs += [pl.BlockSpec((1, 1, d), lambda i: ((i * tm) // tokens_per_mod, 0, 0))]
    in_specs += [pl.BlockSpec(v.shape, lambda i: (0, 0)) for v in vecs]
    in_specs += [pl.BlockSpec(w.shape, lambda i: (0, 0))]
    return pl.pallas_call(
        kern,
        grid=(n // tm,),
        in_specs=in_specs,
        out_specs=pl.BlockSpec((tm, d), lambda i: (i, 0)),
        out_shape=jax.ShapeDtypeStruct((n, d), F32),
        compiler_params=_cparams(1),
        name="mix_out",
    )(x, *[a for a, _, _ in tok_ins], gate_mod, *vecs, w)


def _lora_kernel(x_ref, w_ref, b_ref, o_ref, *, pre, post):
    x = x_ref[...]
    if pre == "tanh":
        x = jnp.tanh(x)
    elif pre == "sigmoid":
        x = jax.nn.sigmoid(x)
    y = _bdot(x, w_ref[...]) + b_ref[...]
    if post == "logsig_gla":
        y = -_softplus(-y) / GLA_GATE_NORM
    elif post == "rwkv_w":
        y = -jnp.exp(-_softplus(-y) - 0.5)
    elif post == "sigmoid":
        y = jax.nn.sigmoid(y)
    o_ref[...] = y


def _lora(x, w, bias, pre=None, post=None, tm=1024):
    n, r = x.shape
    n_out = w.shape[1]
    tm = min(tm, n)
    assert n % tm == 0
    if bias is None:
        bias = jnp.zeros((n_out,), F32)
    return pl.pallas_call(
        functools.partial(_lora_kernel, pre=pre, post=post),
        grid=(n // tm,),
        in_specs=[pl.BlockSpec((tm, r), lambda i: (i, 0)),
                  pl.BlockSpec((r, n_out), lambda i: (0, 0)),
                  pl.BlockSpec((1, n_out), lambda i: (0, 0))],
        out_specs=pl.BlockSpec((tm, n_out), lambda i: (i, 0)),
        out_shape=jax.ShapeDtypeStruct((n, n_out), F32),
        compiler_params=_cparams(1),
        name="lora",
    )(x, w, bias.reshape(1, n_out))


def _rotary_tables(t, dk):
    rows = t // GRID_W
    row = jnp.broadcast_to(jnp.arange(rows, dtype=F32)[:, None], (rows, GRID_W)).reshape(t)
    col = jnp.broadcast_to(jnp.arange(GRID_W, dtype=F32)[None, :], (rows, GRID_W)).reshape(t)
    quarter = dk // 4
    inv = ROPE_BASE ** (-jnp.arange(quarter, dtype=F32) / quarter)
    ang = jnp.concatenate([row[:, None] * inv, col[:, None] * inv], axis=-1)
    cos, sin = jnp.cos(ang), jnp.sin(ang)
    return jnp.concatenate([cos, cos], axis=-1), jnp.concatenate([-sin, sin], axis=-1)


def _retention_log_decay(exp0):
    h = jnp.arange(RET_HEADS, dtype=F32)
    return jnp.log1p(-jnp.power(2.0, -(exp0 + h)))


def _mixer_gla_ret(x, z, b, t, states, latent, gate_mod, tokens_per_mod, w_out, gk_up_f, gk_b_f, gk_up_b, gk_b_b,
                   gla_norm, ret_norm):
    nt = t // min(TIME_TILE, t)
    ncs = min(TIME_TILE, t) // CHUNK
    s_gf, s_gb, s_rf, s_rb = states if states is not None else (None,) * 4
    la_f = _lora(z[:, L0_GDF:L0_GDF + GLA_LOWRANK], gk_up_f, gk_b_f, post="logsig_gla")
    la_b = _lora(z[:, L0_GDB:L0_GDB + GLA_LOWRANK], gk_up_b, gk_b_b, post="logsig_gla")
    gla_in = lambda la: [(z, GLA_DK, L0_GQ), (z, GLA_DK, L0_GK), (z, GLA_DV, L0_GV), (la, GLA_DK, 0)]
    og_f, n_gf = _scan_call("gla", gla_in(la_f), [], s_gf, False, b, t, GLA_HEADS, GLA_DK, GLA_DV)
    og_b, n_gb = _scan_call("gla", gla_in(la_b), [], s_gb, True, b, t, GLA_HEADS, GLA_DK, GLA_DV)
    ones = jnp.ones((b, RET_HEADS, t), F32)
    rows = _row_spec(RET_HEADS, ncs, nt)
    extra = lambda exp0: [(_row_layout(ones * _retention_log_decay(exp0)[None, :, None]), rows),
                          (_row_layout(ones), rows)]
    rot = []
    if latent:
        cos2, sin2 = _rotary_tables(t, RET_DK)
        tab = lambda tile: pl.BlockSpec((min(TIME_TILE, t), RET_DK), lambda bi, hi, ti: (tile(ti), 0))
        rot = [(cos2, tab), (sin2, tab)]
    ret_in = [(z, RET_DK, L0_RQ), (z, RET_DK, L0_RK), (z, RET_DV, L0_RV)]
    kw = dict(delta=False, rotary=latent)
    or_f, n_rf = _scan_call("scalar", ret_in, extra(RET_DECAY_EXP_FWD) + rot, s_rf, False, b, t, RET_HEADS,
                            RET_DK, RET_DV, **kw)
    or_b, n_rb = _scan_call("scalar", ret_in, extra(RET_DECAY_EXP_BWD) + rot, s_rb, True, b, t, RET_HEADS,
                            RET_DK, RET_DV, **kw)
    toks = [(og_f, GLA_V, 0), (og_b, GLA_V, 0), (or_f, RET_V, 0), (or_b, RET_V, 0),
            (z, GLA_V, L0_GG // GLA_V), (z, RET_V, L0_RG // RET_V)]
    vecs = [jnp.tile(gla_norm, GLA_HEADS)[None, :], jnp.tile(ret_norm, RET_HEADS)[None, :]]
    x = _mix_out(_mix_out_l0_kernel, x, toks, gate_mod, tokens_per_mod, vecs, w_out)
    return x, (n_gf, n_gb, n_rf, n_rb)


def _centred_dwconv(x, w):
    t = x.shape[1]
    xp = jnp.pad(x, ((0, 0), (CONV_K // 2, CONV_K // 2), (0, 0)))
    out = xp[:, 0:t] * w[0]
    for j in range(1, CONV_K):
        out = out + xp[:, j:j + t] * w[j]
    return out


def _centred_shift_mix(z, mu):
    zp = jnp.pad(z, ((0, 0), (1, 1), (0, 0)))
    shifted = 0.5 * (zp[:, :-2] + zp[:, 2:])
    return z + mu * (shifted - z)


def _mixer_gdn_rwkv(x, z, b, t, states, gate_mod, tokens_per_mod, w_out, conv_w, A_log_f, dt_bias_f, A_log_b,
                    dt_bias_b, gdn_norm, mu, w0_f, w2_f, a0_f, a2_f, w0_b, w2_b, a0_b, a2_b, g2, k_k, k_a, r_k,
                    ln_w, ln_b):
    n = b * t
    nt = t // min(TIME_TILE, t)
    ncs = min(TIME_TILE, t) // CHUNK
    s_df, s_db, s_wf, s_wb = states if states is not None else (None,) * 4
    zb = z.reshape(b, t, -1)
    qkv = _silu(_centred_dwconv(zb[..., :GDN_QKV], conv_w)).reshape(n, GDN_QKV)
    a_f, a_b, b_f, b_b = jnp.split(jnp.swapaxes(zb[..., L1_SCAL:L1_END], 1, 2), 4, axis=1)
    la_f = -jnp.exp(A_log_f)[:, None] * jax.nn.softplus(a_f + dt_bias_f[:, None])
    la_b = -jnp.exp(A_log_b)[:, None] * jax.nn.softplus(a_b + dt_bias_b[:, None])
    rows = _row_spec(4, ncs, nt)
    gdn_in = [(qkv, GDN_DK, 0), (qkv, GDN_DK, GDN_QK), (qkv, GDN_DV, 2 * GDN_QK)]
    kw = dict(delta=True, rotary=False)
    od_f, n_df = _scan_call("scalar", gdn_in, [(_row_layout(la_f), rows), (_row_layout(jax.nn.sigmoid(b_f)), rows)],
                            s_df, False, b, t, GDN_HEADS, GDN_DK, GDN_DV, **kw)
    od_b, n_db = _scan_call("scalar", gdn_in, [(_row_layout(la_b), rows), (_row_layout(jax.nn.sigmoid(b_b)), rows)],
                            s_db, True, b, t, GDN_HEADS, GDN_DK, GDN_DV, **kw)
    zr = _centred_shift_mix(zb[..., L1_RWKV:L1_RWKV + RWKV_IN], mu).reshape(n, RWKV_IN)
    o_wd = 3 * RWKV_C
    o_ad = o_wd + 2 * RWKV_DECAY_LORA
    o_gd = o_ad + 2 * RWKV_AAA_LORA
    vec = lambda v: (v.reshape(1, RWKV_C),
                     lambda tile: pl.BlockSpec((1, 4 * RWKV_N), lambda bi, hi, ti: (0, hi)))
    outs = []
    for rev, (w0, w2, a0, a2), s0, o in ((False, (w0_f, w2_f, a0_f, a2_f), s_wf, 0),
                                       (True, (w0_b, w2_b, a0_b, a2_b), s_wb, 1)):
        lw = _lora(zr[:, o_wd + o * RWKV_DECAY_LORA:o_wd + (o + 1) * RWKV_DECAY_LORA], w2, w0,
                   pre="tanh", post="rwkv_w")
        asig = _lora(zr[:, o_ad + o * RWKV_AAA_LORA:o_ad + (o + 1) * RWKV_AAA_LORA], a2, a0, post="sigmoid")
        toks = [(zr, RWKV_N, 0), (zr, RWKV_N, RWKV_C), (zr, RWKV_N, 2 * RWKV_C), (lw, RWKV_N, 0), (asig, RWKV_N, 0)]
        outs.append(_scan_call("rwkv", toks, [vec(k_k), vec(k_a), vec(r_k)], s0, rev, b, t, RWKV_HEADS, RWKV_N,
                               RWKV_N, n_tok_out=2))
    (ow_f, bon_f, n_wf), (ow_b, bon_b, n_wb) = outs
    gate = _lora(zr[:, o_gd:o_gd + RWKV_GATE_LORA], g2, None, pre="sigmoid")
    toks = [(od_f, GDN_V, 0), (od_b, GDN_V, 0), (ow_f, RWKV_C, 0), (ow_b, RWKV_C, 0), (bon_f, RWKV_C, 0),
            (bon_b, RWKV_C, 0), (z, GDN_V, L1_GG // GDN_V), (gate, RWKV_C, 0)]
    vecs = [jnp.tile(gdn_norm, GDN_HEADS)[None, :], ln_w[None, :], ln_b[None, :]]
    x = _mix_out(_mix_out_l1_kernel, x, toks, gate_mod, tokens_per_mod, vecs, w_out)
    return x, (n_df, n_db, n_wf, n_wb)


def _permuted_w_in(w, pieces, total):
    cols = [w[:, a:b] for a, b in pieces]
    pad = (-total) % PROJ_TN
    if pad:
        cols.append(jnp.zeros((w.shape[0], pad), w.dtype))
    return jnp.concatenate(cols, axis=1).astype(BF16)


def kernel(x_prompt, x_sample, c, c_ctx, state_l0_gla_fwd, state_l0_gla_bwd, state_l0_ret_fwd, state_l0_ret_bwd, state_l1_gdn_fwd, state_l1_gdn_bwd, state_l1_rwkv_fwd, state_l1_rwkv_bwd, l0_w_mod, l0_b_mod, l0_norm1, l0_norm2, l0_norm3, l0_ffn1_wg, l0_ffn1_wu, l0_ffn1_wd, l0_ffn2_wg, l0_ffn2_wu, l0_ffn2_wd, l0_w_in, l0_w_out, l0_gla_gk_up_fwd, l0_gla_gk_b_fwd, l0_gla_gk_up_bwd, l0_gla_gk_b_bwd, l0_gla_norm, l0_ret_norm, l1_w_mod, l1_b_mod, l1_norm1, l1_norm2, l1_norm3, l1_ffn1_wg, l1_ffn1_wu, l1_ffn1_wd, l1_ffn2_wg, l1_ffn2_wu, l1_ffn2_wd, l1_w_in, l1_w_out, l1_gdn_conv, l1_gdn_A_log_fwd, l1_gdn_dt_bias_fwd, l1_gdn_A_log_bwd, l1_gdn_dt_bias_bwd, l1_gdn_norm, l1_rwkv_mu, l1_rwkv_w0_fwd, l1_rwkv_w2_fwd, l1_rwkv_a0_fwd, l1_rwkv_a2_fwd, l1_rwkv_w0_bwd, l1_rwkv_w2_bwd, l1_rwkv_a0_bwd, l1_rwkv_a2_bwd, l1_rwkv_g2, l1_rwkv_k_k, l1_rwkv_k_a, l1_rwkv_r_k, l1_rwkv_ln_w, l1_rwkv_ln_b, final_norm):
    bp, tp, d = x_prompt.shape
    bs, ts, _ = x_sample.shape
    cond = jnp.concatenate([c_ctx[None, :], c, jnp.zeros((8 - 1 - bs, d), F32)], axis=0)
    common = (
        (l0_w_mod, l0_b_mod, (l0_norm1, l0_norm2, l0_norm3), (l0_ffn1_wg, l0_ffn1_wu, l0_ffn1_wd),
         (l0_ffn2_wg, l0_ffn2_wu, l0_ffn2_wd), l0_w_in, l0_w_out),
        (l1_w_mod, l1_b_mod, (l1_norm1, l1_norm2, l1_norm3), (l1_ffn1_wg, l1_ffn1_wu, l1_ffn1_wd),
         (l1_ffn2_wg, l1_ffn2_wu, l1_ffn2_wd), l1_w_in, l1_w_out),
    )
    caches = (
        (state_l0_gla_fwd, state_l0_gla_bwd, state_l0_ret_fwd, state_l0_ret_bwd),
        (state_l1_gdn_fwd, state_l1_gdn_bwd, state_l1_rwkv_fwd, state_l1_rwkv_bwd),
    )
    l0_pieces = [(0, 2 * GLA_QK + 2 * GLA_V), (L0_END - 2 * RET_QK - 2 * RET_V, L0_END),
                 (2 * GLA_QK + 2 * GLA_V, 2 * GLA_QK + 2 * GLA_V + 2 * GLA_LOWRANK)]
    l1_pieces = [(0, GDN_QKV + GDN_V), (GDN_IN, GDN_IN + RWKV_IN), (GDN_QKV + GDN_V, GDN_IN)]
    w_in_perm = (_permuted_w_in(l0_w_in, l0_pieces, L0_END), _permuted_w_in(l1_w_in, l1_pieces, L1_END))
    l0_params = (l0_gla_gk_up_fwd, l0_gla_gk_b_fwd, l0_gla_gk_up_bwd, l0_gla_gk_b_bwd, l0_gla_norm, l0_ret_norm)
    l1_params = (l1_gdn_conv, l1_gdn_A_log_fwd, l1_gdn_dt_bias_fwd, l1_gdn_A_log_bwd, l1_gdn_dt_bias_bwd,
                 l1_gdn_norm, l1_rwkv_mu, l1_rwkv_w0_fwd, l1_rwkv_w2_fwd, l1_rwkv_a0_fwd, l1_rwkv_a2_fwd,
                 l1_rwkv_w0_bwd, l1_rwkv_w2_bwd, l1_rwkv_a0_bwd, l1_rwkv_a2_bwd,
                 l1_rwkv_g2, l1_rwkv_k_k, l1_rwkv_k_a, l1_rwkv_r_k, l1_rwkv_ln_w, l1_rwkv_ln_b)
    bf = lambda w: w.astype(BF16)
    xs = [x_prompt.reshape(bp * tp, d), x_sample.reshape(bs * ts, d)]
    geom = [(bp, tp, bp * tp, slice(0, 1)), (bs, ts, ts, slice(1, 1 + bs))]
    new_states = []
    for layer in range(2):
        w_mod, b_mod, norms, ffn1, ffn2, _, w_out = common[layer]
        m = _adaln(cond, w_mod, b_mod)
        mods = [t_.reshape(8, 1, d) for t_ in jnp.split(m, N_MOD, axis=-1)]
        ffn1 = tuple(bf(w) for w in ffn1)
        ffn2 = tuple(bf(w) for w in ffn2)
        w_out = bf(w_out)
        for gi in range(2):
            b, t, tpm, rows = geom[gi]
            sh1, sc1, g1, sh2, sc2, g2, sh3, sc3, g3 = [mm[rows] for mm in mods]
            x = _ffn(xs[gi], (sh1, sc1, g1), tpm, norms[0], *ffn1, final_norm, False)
            z = _proj_in(x, sh2, sc2, tpm, norms[1], w_in_perm[layer])
            cache = caches[layer] if gi == 1 else None
            if layer == 0:
                x, st = _mixer_gla_ret(x, z, b, t, cache, gi == 1, g2, tpm, w_out, *l0_params)
            else:
                x, st = _mixer_gdn_rwkv(x, z, b, t, cache, g2, tpm, w_out, *l1_params)
            if gi == 0:
                new_states.extend(st)
            xs[gi] = _ffn(x, (sh3, sc3, g3), tpm, norms[2], *ffn2, final_norm, layer == 1)
    return (xs[0].reshape(bp, tp, d), xs[1].reshape(bs, ts, d), *new_states)
```

```python
import functools

import numpy as np
import jax
import jax.numpy as jnp
from jax import lax
from jax.experimental import pallas as pl
from jax.experimental.pallas import tpu as pltpu

F32 = jnp.float32
BF16 = jnp.bfloat16

D_MODEL = 2048
D_FF = 5632
N_MOD = 9
NORM_EPS = 1e-6
GRID_W = 64
ROPE_BASE = 10000.0
CHUNK = 64
SUB = 16
TIME_TILE = 256
L1_HEADS_PER_STEP = 8

GLA_HEADS, GLA_DK, GLA_DV, GLA_LOWRANK, GLA_GATE_NORM = 4, 128, 256, 16, 16.0
GLA_QK, GLA_V = GLA_HEADS * GLA_DK, GLA_HEADS * GLA_DV
RET_HEADS, RET_DK, RET_DV = 4, 128, 256
RET_QK, RET_V = RET_HEADS * RET_DK, RET_HEADS * RET_DV
RET_DECAY_EXP_FWD, RET_DECAY_EXP_BWD = 5.0, 5.5
GDN_HEADS, GDN_DK, GDN_DV = 8, 128, 128
GDN_QK, GDN_V = GDN_HEADS * GDN_DK, GDN_HEADS * GDN_DV
GDN_QKV = 2 * GDN_QK + GDN_V
CONV_K = 5
GDN_IN = GDN_QKV + GDN_V + 4 * GDN_HEADS
RWKV_HEADS, RWKV_N = 16, 64
RWKV_C = RWKV_HEADS * RWKV_N
RWKV_DECAY_LORA, RWKV_AAA_LORA, RWKV_GATE_LORA = 64, 64, 128
RWKV_GN_EPS = 64e-5
RWKV_IN = 3 * RWKV_C + 2 * RWKV_DECAY_LORA + 2 * RWKV_AAA_LORA + RWKV_GATE_LORA

L0_GQ, L0_GK, L0_GV, L0_GG = 0, GLA_QK, 2 * GLA_QK, 2 * GLA_QK + GLA_V
L0_RQ = L0_GG + GLA_V
L0_RK, L0_RV = L0_RQ + RET_QK, L0_RQ + 2 * RET_QK
L0_RG = L0_RV + RET_V
L0_GDF = L0_RG + RET_V
L0_GDB = L0_GDF + GLA_LOWRANK
L0_END = L0_GDB + GLA_LOWRANK
L1_GG = GDN_QKV
L1_LORA = GDN_QKV + GDN_V
RWKV_LORA_IN = RWKV_IN - 3 * RWKV_C
L1_SCAL = L1_LORA + RWKV_LORA_IN
PROJ_TN = 512
L1_R = L1_LORA + PROJ_TN
L1_END = L1_R + 3 * RWKV_C
FILTER_BW = 512

VMEM_LIMIT_BYTES = 56 * 1024 * 1024


def _cparams(n_axes):
    return pltpu.CompilerParams(dimension_semantics=("arbitrary",) * n_axes, vmem_limit_bytes=VMEM_LIMIT_BYTES)


def _bdot(a, b):
    return jnp.dot(a.astype(BF16), b.astype(BF16), preferred_element_type=F32)


def _bmm(a, b):
    return lax.dot_general(a.astype(BF16), b.astype(BF16), (((2,), (1,)), ((0,), (0,))),
                           preferred_element_type=F32)


def _bmm_nt(a, b):
    return lax.dot_general(a.astype(BF16), b.astype(BF16), (((2,), (2,)), ((0,), (0,))),
                           preferred_element_type=F32)


def _bmm_tn(a, b):
    return lax.dot_general(a.astype(BF16), b.astype(BF16), (((1,), (1,)), ((0,), (0,))),
                           preferred_element_type=F32)


def _iota2(n, m, axis):
    return lax.broadcasted_iota(jnp.int32, (n, m), axis)


def _tile_cumsum(x, rev):
    tc = x.shape[0]
    row = _iota2(tc, tc, 0)
    col = _iota2(tc, tc, 1)
    same = jnp.right_shift(row, 6) == jnp.right_shift(col, 6)
    tri = jnp.where(same & ((row >= col) if not rev else (row <= col)), 1.0, 0.0).astype(BF16)
    x1 = x.astype(BF16)
    r1 = x - x1.astype(F32)
    x2 = r1.astype(BF16)
    x3 = (r1 - x2.astype(F32)).astype(BF16)
    d = lambda y: jnp.dot(tri, y, preferred_element_type=F32)
    return d(x1) + (d(x2) + d(x3))


def _chunk_masks(c, rev):
    row = _iota2(c, c, 0)
    col = _iota2(c, c, 1)
    eye = row == col
    incl = (row >= col) if not rev else (row <= col)
    strict = (row > col) if not rev else (row < col)
    same_blk = jnp.right_shift(row, 4) == jnp.right_shift(col, 4)
    return eye, incl, strict, same_blk


def _l2norm(x):
    return x * lax.rsqrt(jnp.sum(x * x, axis=-1, keepdims=True) + NORM_EPS)


def _silu(x):
    return x * jax.nn.sigmoid(x)


def _softplus(x):
    return jnp.maximum(x, 0.0) + jnp.log1p(jnp.exp(-jnp.abs(x)))


def _split_heads(x, hb, ncs):
    d = x.shape[1] // hb
    return jnp.concatenate([x[:, h * d:(h + 1) * d].reshape(ncs, CHUNK, d) for h in range(hb)], axis=0)


def _merge_heads(y, hb, ncs):
    d = y.shape[-1]
    return jnp.concatenate([y[h * ncs:(h + 1) * ncs].reshape(ncs * CHUNK, d) for h in range(hb)], axis=1)


def _gla_phase1(q, k, v, g, rev):
    nb, c, kd = q.shape
    row = _iota2(c, c, 0)
    col = _iota2(c, c, 1)
    ns = c // SUB
    rows = []
    for i in range(ns):
        first = (i == 0) if not rev else (i == ns - 1)
        if first:
            rows.append(jnp.zeros((nb, SUB, c), F32))
            continue
        r0 = i * SUB
        gref = g[:, r0 - 1:r0, :] if not rev else g[:, r0 + SUB:r0 + SUB + 1, :]
        qs = q[:, r0:r0 + SUB, :] * jnp.exp(g[:, r0:r0 + SUB, :] - gref)
        ks = k * jnp.exp(jnp.minimum(gref - g, 0.0))
        rows.append(_bmm_nt(qs, ks))
    att = jnp.concatenate(rows, axis=1)
    blk_r = jnp.right_shift(row, 4)
    blk_c = jnp.right_shift(col, 4)
    att = jnp.where((blk_r > blk_c) if not rev else (blk_r < blk_c), att, 0.0)
    dist = (row - col) if not rev else (col - row)
    dist = jnp.where(blk_r == blk_c, dist, -1)
    k2 = k.reshape(nb * c, kd)
    g2 = g.reshape(nb * c, kd)
    for d in range(SUB):
        if d == 0:
            ksh, gsh = k, g
        else:
            sh = d if not rev else nb * c - d
            ksh = pltpu.roll(k2, sh, 0).reshape(nb, c, kd)
            gsh = pltpu.roll(g2, sh, 0).reshape(nb, c, kd)
        e = jnp.exp(jnp.minimum(g - gsh, 0.0))
        band = jnp.sum(q * ksh * e, axis=2, keepdims=True)
        att = jnp.where(dist == d, band, att)
    g_end = g[:, c - 1:c, :] if not rev else g[:, 0:1, :]
    eye_k = _iota2(kd, kd, 0) == _iota2(kd, kd, 1)
    dcol = jnp.sum(jnp.where(eye_k, jnp.exp(g_end), 0.0), axis=2, keepdims=True)
    return q * jnp.exp(g), _bmm(att, v), _bmm_tn(k * jnp.exp(g_end - g), v), dcol


def _gla_kernel(*refs, rev, has_s0, hb, ncs):
    q_ref, k_ref, v_ref, la_ref = refs[:4]
    pos = 4
    s0_ref = None
    if has_s0:
        s0_ref = refs[pos]
        pos += 1
    o_ref, sout_ref, s_scr = refs[pos], refs[pos + 1], refs[pos + 2]
    _init_state(s_scr, s0_ref)
    sp = lambda x: _split_heads(x, hb, ncs)
    r2, y0, s_add, dec = _gla_phase1(sp(q_ref[...] * GLA_DK ** -0.5), sp(k_ref[...]), sp(v_ref[...]),
                                     sp(_tile_cumsum(la_ref[...], rev)), rev)
    s = _phase2(r2, y0, None, 1.0, s_add, dec, s_scr[...], o_ref, hb, ncs, rev)
    s_scr[...] = s

    @pl.when(pl.program_id(2) == pl.num_programs(2) - 1)
    def _():
        sout_ref[0] = s


def _tri_inverse_b(n_mat, eye, same_blk):
    eyef = jnp.where(eye, 1.0, 0.0)
    nd = jnp.where(same_blk, n_mat, 0.0)
    no = n_mat - nd
    p = eyef - nd
    m = nd
    for _ in range(3):
        m = _bmm(m, m)
        p = p + _bmm(p, m)
    mm = _bmm(p, no)
    q = eyef - mm
    q = q + _bmm(q, _bmm(mm, mm))
    return _bmm(q, p)


def _rwkv_phase1(r, lw, g, k, v, a, b, rev):
    nb, c, n = r.shape
    eye, incl, strict, same_blk = _chunk_masks(c, rev)
    eg = jnp.exp(g)
    einv = jnp.exp(-g)
    at = a * jnp.exp(g - lw)
    rt = r * eg
    ar = jnp.concatenate([at, rt], axis=1)
    ab = _bmm_nt(ar, b * einv)
    ak = _bmm_nt(ar, k * einv)
    a_ab = jnp.where(strict, ab[:, :c], 0.0)
    a_rb = jnp.where(incl, ab[:, c:], 0.0)
    a_ak = jnp.where(strict, ak[:, :c], 0.0)
    a_rk = jnp.where(incl, ak[:, c:], 0.0)
    t_inv = _tri_inverse_b(-a_ab, eye, same_blk)
    w = _bmm(t_inv, at)
    u0 = _bmm(t_inv, _bmm(a_ak, v))
    r2 = rt + _bmm(a_rb, w)
    y0 = _bmm(a_rk, v) + _bmm(a_rb, u0)
    g_end = g[:, c - 1:c, :] if not rev else g[:, 0:1, :]
    dec = jnp.exp(g_end - g)
    bd = b * dec
    m = _bmm_tn(bd, w)
    s_add = _bmm_tn(bd, u0) + _bmm_tn(k * dec, v)
    eye_n = _iota2(n, n, 0) == _iota2(n, n, 1)
    dcol = jnp.sum(jnp.where(eye_n, jnp.exp(g_end), 0.0), axis=2, keepdims=True)
    return r2, y0, m, s_add, dcol


def _scalar_phase1(q, k, v, lc_row, beta_row, rev, delta):
    nb, c, _ = q.shape
    eye, incl, strict, same_blk = _chunk_masks(c, rev)
    row = _iota2(c, c, 0)
    col = _iota2(c, c, 1)
    before = (row <= col) if not rev else (row >= col)
    to_col = lambda x: jnp.sum(jnp.where(eye, x, 0.0), axis=2, keepdims=True)
    lc_col = to_col(lc_row)
    g_row = jnp.sum(jnp.where(before, lc_col, 0.0), axis=1, keepdims=True)
    g_col = to_col(g_row)
    rel = jnp.where(incl, jnp.exp(jnp.minimum(g_col - g_row, 0.0)), 0.0)
    eg = jnp.exp(g_col)
    g_end = g_col[:, c - 1:c, :] if not rev else g_col[:, 0:1, :]
    kdec = k * jnp.exp(g_end - g_col)
    if delta:
        beta_col = to_col(beta_row)
        qkk = _bmm_nt(jnp.concatenate([q, k], axis=1), k)
        p = qkk[:, :c] * rel
        n_mat = jnp.where(strict, beta_col * rel * qkk[:, c:], 0.0)
        t_inv = _tri_inverse_b(n_mat, eye, same_blk)
        sol_v = _bmm(t_inv, beta_col * v)
        sol_k = _bmm(t_inv, (beta_col * eg) * k)
        r2 = q * eg - _bmm(p, sol_k)
        y0 = _bmm(p, sol_v)
        m = _bmm_tn(kdec, sol_k)
        s_add = _bmm_tn(kdec, sol_v)
    else:
        p = _bmm_nt(q, k) * rel
        r2 = q * eg
        y0 = _bmm(p, v)
        m = None
        s_add = _bmm_tn(kdec, v)
    return r2, y0, m, s_add, jnp.exp(g_end)


def _phase2(r2, y0, m, m_sign, s_add, dec, s, o_ref, hb, ncs, rev):
    unb = lambda x: x.reshape((hb, ncs) + x.shape[1:])
    r2, y0, s_add, dec = unb(r2), unb(y0), unb(s_add), unb(dec)
    if m is not None:
        m = unb(m)
    for ci in range(ncs):
        cc = (ncs - 1 - ci) if rev else ci
        y = y0[:, cc] + _bmm(r2[:, cc], s)
        o_ref[cc * CHUNK:(cc + 1) * CHUNK, :] = jnp.concatenate([y[h] for h in range(hb)], axis=1)
        s_new = dec[:, cc] * s + s_add[:, cc]
        if m is not None:
            s_new = s_new + m_sign * _bmm(m[:, cc], s)
        s = s_new
    return s


def _init_state(s_scr, s0_ref):
    @pl.when(pl.program_id(2) == 0)
    def _():
        if s0_ref is not None:
            s_scr[...] = s0_ref[0]
        else:
            s_scr[...] = jnp.zeros(s_scr.shape, F32)


def _scalar_kernel(*refs, rev, has_s0, hb, ncs, delta, rotary):
    q_ref, k_ref, v_ref, la_ref, be_ref = refs[:5]
    pos = 5
    cos_ref = sin_ref = s0_ref = None
    if rotary:
        cos_ref, sin_ref = refs[pos], refs[pos + 1]
        pos += 2
    if has_s0:
        s0_ref = refs[pos]
        pos += 1
    o_ref, sout_ref, s_scr = refs[pos], refs[pos + 1], refs[pos + 2]
    _init_state(s_scr, s0_ref)
    nb = hb * ncs
    q = q_ref[...]
    k = k_ref[...]
    dk = q.shape[1] // hb
    if rotary:
        cos2, sin2 = cos_ref[...], sin_ref[...]
        rot = lambda x: jnp.concatenate(
            [x[:, h * dk:(h + 1) * dk] * cos2 + pltpu.roll(x[:, h * dk:(h + 1) * dk], dk // 2, 1) * sin2
             for h in range(hb)], axis=1)
        q, k = rot(q), rot(k)
    q = _split_heads(q, hb, ncs)
    k = _split_heads(k, hb, ncs)
    if delta:
        q, k = _l2norm(q), _l2norm(k)
    q = q * dk ** -0.5
    v = _split_heads(v_ref[...], hb, ncs)
    r2, y0, m, s_add, dec = _scalar_phase1(q, k, v, la_ref[0].reshape(nb, 1, CHUNK),
                                           be_ref[0].reshape(nb, 1, CHUNK), rev, delta)
    s = _phase2(r2, y0, m, -1.0, s_add, dec, s_scr[...], o_ref, hb, ncs, rev)
    s_scr[...] = s

    @pl.when(pl.program_id(2) == pl.num_programs(2) - 1)
    def _():
        sout_ref[0] = s


def _rwkv_kernel(*refs, rev, has_s0, hb, ncs):
    r_ref, kr_ref, v_ref, lw_ref, a_ref, kk_ref, ka_ref, rk_ref = refs[:8]
    pos = 8
    s0_ref = None
    if has_s0:
        s0_ref = refs[pos]
        pos += 1
    o_ref, bonus_ref, sout_ref, s_scr = refs[pos], refs[pos + 1], refs[pos + 2], refs[pos + 3]
    _init_state(s_scr, s0_ref)
    r = r_ref[...]
    kr = kr_ref[...]
    a = a_ref[...]
    kd = kr * (1.0 + (a - 1.0) * ka_ref[...])
    sp = lambda x: _split_heads(x, hb, ncs)
    kk = _l2norm(sp(kr * kk_ref[...]))
    a_s = sp(a)
    v_s = sp(v_ref[...])
    bonus = jnp.sum(sp(r * kd * rk_ref[...]), axis=-1, keepdims=True) * v_s
    bonus_ref[...] = _merge_heads(bonus, hb, ncs)
    lw = lw_ref[...]
    r2, y0, m, s_add, dec = _rwkv_phase1(sp(r), sp(lw), sp(_tile_cumsum(lw, rev)), sp(kd), v_s, -kk, kk * a_s, rev)
    s = _phase2(r2, y0, m, 1.0, s_add, dec, s_scr[...], o_ref, hb, ncs, rev)
    s_scr[...] = s

    @pl.when(pl.program_id(2) == pl.num_programs(2) - 1)
    def _():
        sout_ref[0] = s


def _scan_call(kind, tok_ins, extra_ins, s0, rev, b, t, heads, dk, dv, n_tok_out=1, hb=4, **kw):
    tc = min(TIME_TILE, t)
    nt = t // tc
    ncs = tc // CHUNK
    tile = lambda ti: (nt - 1 - ti) if rev else ti

    def tok_spec(width, off):
        bw = hb * width
        assert off % bw == 0
        cb = off // bw
        return pl.BlockSpec((tc, bw), lambda bi, hi, ti: (bi * nt + tile(ti), cb + hi))

    in_specs = [tok_spec(w, off) for _, w, off in tok_ins] + [mk(tile) for _, mk in extra_ins]
    args = [x for x, _, _ in tok_ins] + [x for x, _ in extra_ins]
    state_spec = pl.BlockSpec((1, hb, dk, dv), lambda bi, hi, ti: (bi, hi, 0, 0))
    if s0 is not None:
        in_specs.append(state_spec)
        args.append(s0)
    body = {"gla": _gla_kernel, "scalar": _scalar_kernel, "rwkv": _rwkv_kernel}[kind]
    outs = pl.pallas_call(
        functools.partial(body, rev=rev, has_s0=s0 is not None, hb=hb, ncs=ncs, **kw),
        grid=(b, heads // hb, nt),
        in_specs=in_specs,
        out_specs=[tok_spec(dv, 0)] * n_tok_out + [state_spec],
        out_shape=[jax.ShapeDtypeStruct((b * t, heads * dv), F32)] * n_tok_out
        + [jax.ShapeDtypeStruct((b, heads, dk, dv), F32)],
        scratch_shapes=[pltpu.VMEM((hb, dk, dv), F32)],
        compiler_params=_cparams(3),
        name=f"scan_{kind}_{'bwd' if rev else 'fwd'}",
    )(*args)
    return outs


def _row_spec(hb, ncs, nt):
    return lambda tile: pl.BlockSpec((1, hb, ncs, 1, CHUNK), lambda bi, hi, ti: (bi, hi, tile(ti), 0, 0))


def _row_layout(x):
    b, h, t = x.shape
    return x.reshape(b, h, t // CHUNK, 1, CHUNK)


def _modulated_norm(x, nw, shift, scale):
    y = x * lax.rsqrt(jnp.mean(x * x, axis=-1, keepdims=True) + NORM_EPS) * nw
    return y * (1.0 + scale) + shift


def _adaln_kernel(c_ref, w_ref, b_ref, o_ref):
    o_ref[...] = _bdot(_silu(c_ref[...]), w_ref[...]) + b_ref[...]


def _adaln(cond, w_mod, b_mod):
    r, d = cond.shape
    n = w_mod.shape[1]
    tn = 1024
    return pl.pallas_call(
        _adaln_kernel,
        grid=(n // tn,),
        in_specs=[pl.BlockSpec((r, d), lambda j: (0, 0)),
                  pl.BlockSpec((d, tn), lambda j: (0, j)),
                  pl.BlockSpec((1, tn), lambda j: (0, j))],
        out_specs=pl.BlockSpec((r, tn), lambda j: (0, j)),
        out_shape=jax.ShapeDtypeStruct((r, n), F32),
        compiler_params=_cparams(1),
        name="adaln",
    )(cond, w_mod, b_mod.reshape(1, n))


def _ffn_kernel(x_ref, sh_ref, sc_ref, g_ref, nw_ref, wg_ref, wu_ref, wd_ref, fn_ref, o_ref, h_scr, *,
                nf, final_norm):
    f = pl.program_id(1)

    @pl.when(f == 0)
    def _():
        h = _modulated_norm(x_ref[...], nw_ref[...], sh_ref[0], sc_ref[0])
        h_scr[...] = h.astype(BF16)
        o_ref[...] = jnp.zeros(o_ref.shape, F32)

    h = h_scr[...]
    gate = jnp.dot(h, wg_ref[...], preferred_element_type=F32)
    up = jnp.dot(h, wu_ref[...], preferred_element_type=F32)
    o_ref[...] += jnp.dot((_silu(gate) * up).astype(BF16), wd_ref[...], preferred_element_type=F32)

    @pl.when(f == nf - 1)
    def _():
        y = x_ref[...] + (0.5 * g_ref[0]) * o_ref[...]
        if final_norm:
            y = y * lax.rsqrt(jnp.mean(y * y, axis=-1, keepdims=True) + NORM_EPS) * fn_ref[...]
        o_ref[...] = y


def _ffn(x, mods, tokens_per_mod, nw, wg, wu, wd, fnorm, final_norm, tm=1024, tf=512):
    n, d = x.shape
    shift, scale, gate = mods
    tm = min(tm, tokens_per_mod)
    nf = wg.shape[1] // tf
    mod_spec = pl.BlockSpec((1, 1, d), lambda i, f: ((i * tm) // tokens_per_mod, 0, 0))
    vec_spec = pl.BlockSpec((1, d), lambda i, f: (0, 0))
    return pl.pallas_call(
        functools.partial(_ffn_kernel, nf=nf, final_norm=final_norm),
        grid=(n // tm, nf),
        in_specs=[pl.BlockSpec((tm, d), lambda i, f: (i, 0), pipeline_mode=pl.Buffered(1)),
                  mod_spec, mod_spec, mod_spec, vec_spec,
                  pl.BlockSpec((d, tf), lambda i, f: (0, f)),
                  pl.BlockSpec((d, tf), lambda i, f: (0, f)),
                  pl.BlockSpec((tf, d), lambda i, f: (f, 0)),
                  vec_spec],
        out_specs=pl.BlockSpec((tm, d), lambda i, f: (i, 0)),
        out_shape=jax.ShapeDtypeStruct((n, d), F32),
        scratch_shapes=[pltpu.VMEM((tm, d), BF16)],
        compiler_params=_cparams(2),
        name="ffn",
    )(x, shift, scale, gate, nw.reshape(1, d), wg, wu, wd, fnorm.reshape(1, d))


def _proj_in_kernel(x_ref, sh_ref, sc_ref, nw_ref, w_ref, o_ref, h_scr):
    @pl.when(pl.program_id(1) == 0)
    def _():
        h_scr[...] = _modulated_norm(x_ref[...], nw_ref[...], sh_ref[0], sc_ref[0]).astype(BF16)

    o_ref[...] = jnp.dot(h_scr[...], w_ref[...], preferred_element_type=F32)


def _proj_in(x, shift, scale, tokens_per_mod, nw, w, tm=1024, tn=PROJ_TN):
    n, d = x.shape
    n_out = w.shape[1]
    tm = min(tm, tokens_per_mod)
    mod_spec = pl.BlockSpec((1, 1, d), lambda i, j: ((i * tm) // tokens_per_mod, 0, 0))
    return pl.pallas_call(
        _proj_in_kernel,
        grid=(n // tm, n_out // tn),
        in_specs=[pl.BlockSpec((tm, d), lambda i, j: (i, 0)), mod_spec, mod_spec,
                  pl.BlockSpec((1, d), lambda i, j: (0, 0)),
                  pl.BlockSpec((d, tn), lambda i, j: (0, j))],
        out_specs=pl.BlockSpec((tm, tn), lambda i, j: (i, j)),
        out_shape=jax.ShapeDtypeStruct((n, n_out), F32),
        scratch_shapes=[pltpu.VMEM((tm, d), BF16)],
        compiler_params=_cparams(2),
        name="proj_in",
    )(x, shift, scale, nw.reshape(1, d), w)


def _head_norm_lanes(x, nheads, eps, center):
    d = x.shape[1] // nheads
    outs = []
    for h in range(nheads):
        xh = x[:, h * d:(h + 1) * d]
        if center:
            xh = xh - jnp.mean(xh, axis=-1, keepdims=True)
        outs.append(xh * lax.rsqrt(jnp.mean(xh * xh, axis=-1, keepdims=True) + eps))
    return jnp.concatenate(outs, axis=1)


def _head_norm_half_lanes(x, eps):
    lo = _iota2(1, 128, 1) < 64
    outs = []
    for j in range(x.shape[1] // 128):
        xb = x[:, j * 128:(j + 1) * 128]
        half = lambda y: jnp.where(lo, jnp.sum(jnp.where(lo, y, 0.0), axis=-1, keepdims=True),
                                   jnp.sum(jnp.where(lo, 0.0, y), axis=-1, keepdims=True)) * (1.0 / 64.0)
        xc = xb - half(xb)
        outs.append(xc * lax.rsqrt(half(xc * xc) + eps))
    return jnp.concatenate(outs, axis=1)


def _mix_out_l0_kernel(x_ref, gf_ref, gb_ref, rf_ref, rb_ref, gg_ref, rg_ref, g_ref, gn_ref, rn_ref, w_ref, o_ref):
    og = _head_norm_lanes(gf_ref[...] + gb_ref[...], GLA_HEADS, NORM_EPS, False) * gn_ref[...] * _silu(gg_ref[...])
    orr = _head_norm_lanes(rf_ref[...] + rb_ref[...], RET_HEADS, NORM_EPS, True) * rn_ref[...] * _silu(rg_ref[...])
    u = jnp.concatenate([og, orr], axis=1).astype(BF16)
    o_ref[...] = x_ref[...] + g_ref[0] * jnp.dot(u, w_ref[...], preferred_element_type=F32)


def _mix_out_l1_kernel(x_ref, df_ref, db_ref, wf_ref, wb_ref, bf_ref, bb_ref, gg_ref, gate_ref, g_ref,
                       dn_ref, lnw_ref, lnb_ref, w_ref, o_ref):
    od = _head_norm_lanes(df_ref[...] + db_ref[...], GDN_HEADS, NORM_EPS, False) * dn_ref[...] * _silu(gg_ref[...])
    y = _head_norm_half_lanes(wf_ref[...] + wb_ref[...], RWKV_GN_EPS) * lnw_ref[...] + lnb_ref[...]
    y = (y + bf_ref[...] + bb_ref[...]) * gate_ref[...]
    u = jnp.concatenate([od, y], axis=1).astype(BF16)
    o_ref[...] = x_ref[...] + g_ref[0] * jnp.dot(u, w_ref[...], preferred_element_type=F32)


def _mix_out(kern, x, tok_ins, gate_mod, tokens_per_mod, vecs, w, tm=256):
    n, d = x.shape
    in_specs = [pl.BlockSpec((tm, d), lambda i: (i, 0))]
    in_specs += [pl.BlockSpec((tm, bw), functools.partial(lambda i, cb: (i, cb), cb=cb)) for _, bw, cb in tok_ins]
    in_specs += [pl.BlockSpec((1, 1, d), lambda i: ((i * tm) // tokens_per_mod, 0, 0))]
    in_specs += [pl.BlockSpec(v.shape, lambda i: (0, 0)) for v in vecs]
    in_specs += [pl.BlockSpec(w.shape, lambda i: (0, 0))]
    return pl.pallas_call(
        kern,
        grid=(n // tm,),
        in_specs=in_specs,
        out_specs=pl.BlockSpec((tm, d), lambda i: (i, 0)),
        out_shape=jax.ShapeDtypeStruct((n, d), F32),
        compiler_params=_cparams(1),
        name="mix_out",
    )(x, *[a for a, _, _ in tok_ins], gate_mod, *vecs, w)


def _lora_kernel(x_ref, w_ref, b_ref, o_ref, *, pre, post):
    x = x_ref[...]
    if pre == "tanh":
        x = jnp.tanh(x)
    elif pre == "sigmoid":
        x = jax.nn.sigmoid(x)
    y = _bdot(x, w_ref[...]) + b_ref[...]
    if post == "logsig_gla":
        y = -_softplus(-y) / GLA_GATE_NORM
    elif post == "rwkv_w":
        y = -jnp.exp(-_softplus(-y) - 0.5)
    elif post == "sigmoid":
        y = jax.nn.sigmoid(y)
    o_ref[...] = y


def _lora(x, w, bias, pre=None, post=None, tm=1024):
    n, r = x.shape
    n_out = w.shape[1]
    tm = min(tm, n)
    assert n % tm == 0
    if bias is None:
        bias = jnp.zeros((n_out,), F32)
    return pl.pallas_call(
        functools.partial(_lora_kernel, pre=pre, post=post),
        grid=(n // tm,),
        in_specs=[pl.BlockSpec((tm, r), lambda i: (i, 0)),
                  pl.BlockSpec((r, n_out), lambda i: (0, 0)),
                  pl.BlockSpec((1, n_out), lambda i: (0, 0))],
        out_specs=pl.BlockSpec((tm, n_out), lambda i: (i, 0)),
        out_shape=jax.ShapeDtypeStruct((n, n_out), F32),
        compiler_params=_cparams(1),
        name="lora",
    )(x, w, bias.reshape(1, n_out))


def _rotary_tables(t, dk):
    rows = t // GRID_W
    row = jnp.broadcast_to(jnp.arange(rows, dtype=F32)[:, None], (rows, GRID_W)).reshape(t)
    col = jnp.broadcast_to(jnp.arange(GRID_W, dtype=F32)[None, :], (rows, GRID_W)).reshape(t)
    quarter = dk // 4
    inv = ROPE_BASE ** (-jnp.arange(quarter, dtype=F32) / quarter)
    ang = jnp.concatenate([row[:, None] * inv, col[:, None] * inv], axis=-1)
    cos, sin = jnp.cos(ang), jnp.sin(ang)
    return jnp.concatenate([cos, cos], axis=-1), jnp.concatenate([-sin, sin], axis=-1)


def _retention_log_decay(exp0):
    h = jnp.arange(RET_HEADS, dtype=F32)
    return jnp.log1p(-jnp.power(2.0, -(exp0 + h)))


def _mixer_gla_ret(x, z, b, t, states, latent, gate_mod, tokens_per_mod, w_out, gk_up_f, gk_b_f, gk_up_b, gk_b_b,
                   gla_norm, ret_norm):
    nt = t // min(TIME_TILE, t)
    ncs = min(TIME_TILE, t) // CHUNK
    s_gf, s_gb, s_rf, s_rb = states if states is not None else (None,) * 4
    la_f = _lora(z[:, L0_GDF:L0_GDF + GLA_LOWRANK], gk_up_f, gk_b_f, post="logsig_gla")
    la_b = _lora(z[:, L0_GDB:L0_GDB + GLA_LOWRANK], gk_up_b, gk_b_b, post="logsig_gla")
    gla_in = lambda la: [(z, GLA_DK, L0_GQ), (z, GLA_DK, L0_GK), (z, GLA_DV, L0_GV), (la, GLA_DK, 0)]
    og_f, n_gf = _scan_call("gla", gla_in(la_f), [], s_gf, False, b, t, GLA_HEADS, GLA_DK, GLA_DV)
    og_b, n_gb = _scan_call("gla", gla_in(la_b), [], s_gb, True, b, t, GLA_HEADS, GLA_DK, GLA_DV)
    ones = jnp.ones((b, RET_HEADS, t), F32)
    rows = _row_spec(RET_HEADS, ncs, nt)
    extra = lambda exp0: [(_row_layout(ones * _retention_log_decay(exp0)[None, :, None]), rows),
                          (_row_layout(ones), rows)]
    rot = []
    if latent:
        cos2, sin2 = _rotary_tables(t, RET_DK)
        tab = lambda tile: pl.BlockSpec((min(TIME_TILE, t), RET_DK), lambda bi, hi, ti: (tile(ti), 0))
        rot = [(cos2, tab), (sin2, tab)]
    ret_in = [(z, RET_DK, L0_RQ), (z, RET_DK, L0_RK), (z, RET_DV, L0_RV)]
    kw = dict(delta=False, rotary=latent)
    or_f, n_rf = _scan_call("scalar", ret_in, extra(RET_DECAY_EXP_FWD) + rot, s_rf, False, b, t, RET_HEADS,
                            RET_DK, RET_DV, **kw)
    or_b, n_rb = _scan_call("scalar", ret_in, extra(RET_DECAY_EXP_BWD) + rot, s_rb, True, b, t, RET_HEADS,
                            RET_DK, RET_DV, **kw)
    toks = [(og_f, GLA_V, 0), (og_b, GLA_V, 0), (or_f, RET_V, 0), (or_b, RET_V, 0),
            (z, GLA_V, L0_GG // GLA_V), (z, RET_V, L0_RG // RET_V)]
    vecs = [jnp.tile(gla_norm, GLA_HEADS)[None, :], jnp.tile(ret_norm, RET_HEADS)[None, :]]
    x = _mix_out(_mix_out_l0_kernel, x, toks, gate_mod, tokens_per_mod, vecs, w_out)
    return x, (n_gf, n_gb, n_rf, n_rb)


HALO = 8


def _time_filter_kernel(x_ref, prev_ref, next_ref, p_ref, o_ref, *, mode):
    ti = pl.program_id(1)
    x = x_ref[...]
    tc = x.shape[0]
    prv = jnp.where(ti > 0, prev_ref[...], 0.0)
    nxt = jnp.where(ti < pl.num_programs(1) - 1, next_ref[...], 0.0)
    e = jnp.concatenate([x, nxt, prv], axis=0)
    ne = tc + 2 * HALO
    at = lambda s: pltpu.roll(e, (ne - s) % ne, 0)[:tc]
    p = p_ref[...]
    if mode == "conv":
        acc = x * p[CONV_K // 2:CONV_K // 2 + 1]
        for j in range(CONV_K):
            if j != CONV_K // 2:
                acc = acc + at(j - CONV_K // 2) * p[j:j + 1]
        o_ref[...] = _silu(acc)
    else:
        o_ref[...] = x + p[0:1] * (0.5 * (at(-1) + at(1)) - x)


def _time_filter(z, col0, params, b, t, mode, tc=512):
    n = z.shape[0]
    width = params.shape[1]
    bw = FILTER_BW
    assert col0 % bw == 0 and width % bw == 0
    tc = min(tc, t)
    nt = t // tc
    cb0 = col0 // bw
    hb_rows = tc // HALO
    return pl.pallas_call(
        functools.partial(_time_filter_kernel, mode=mode),
        grid=(b, nt, width // bw),
        in_specs=[pl.BlockSpec((tc, bw), lambda bi, ti, ci: (bi * nt + ti, cb0 + ci)),
                  pl.BlockSpec((HALO, bw), lambda bi, ti, ci: (jnp.maximum((bi * nt + ti) * hb_rows - 1, 0),
                                                               cb0 + ci)),
                  pl.BlockSpec((HALO, bw), lambda bi, ti, ci: (jnp.minimum((bi * nt + ti + 1) * hb_rows,
                                                                           n // HALO - 1), cb0 + ci)),
                  pl.BlockSpec((8, bw), lambda bi, ti, ci: (0, ci))],
        out_specs=pl.BlockSpec((tc, bw), lambda bi, ti, ci: (bi * nt + ti, ci)),
        out_shape=jax.ShapeDtypeStruct((n, width), F32),
        compiler_params=_cparams(3),
        name=f"time_filter_{mode}",
    )(z, z, z, params)


def _mixer_gdn_rwkv(x, z, b, t, states, gate_mod, tokens_per_mod, w_out, conv_w, A_log_f, dt_bias_f, A_log_b,
                    dt_bias_b, gdn_norm, mu, w0_f, w2_f, a0_f, a2_f, w0_b, w2_b, a0_b, a2_b, g2, k_k, k_a, r_k,
                    ln_w, ln_b):
    n = b * t
    nt = t // min(TIME_TILE, t)
    ncs = min(TIME_TILE, t) // CHUNK
    s_df, s_db, s_wf, s_wb = states if states is not None else (None,) * 4
    qkv = _time_filter(z, 0, jnp.pad(conv_w, ((0, 8 - CONV_K), (0, 0))), b, t, "conv")
    scal = z[:, L1_SCAL:L1_SCAL + 4 * GDN_HEADS].reshape(b, t, 4 * GDN_HEADS)
    a_f, a_b, b_f, b_b = jnp.split(jnp.swapaxes(scal, 1, 2), 4, axis=1)
    la_f = -jnp.exp(A_log_f)[:, None] * jax.nn.softplus(a_f + dt_bias_f[:, None])
    la_b = -jnp.exp(A_log_b)[:, None] * jax.nn.softplus(a_b + dt_bias_b[:, None])
    rows = _row_spec(L1_HEADS_PER_STEP, ncs, nt)
    gdn_in = [(qkv, GDN_DK, 0), (qkv, GDN_DK, GDN_QK), (qkv, GDN_DV, 2 * GDN_QK)]
    kw = dict(delta=True, rotary=False, hb=L1_HEADS_PER_STEP)
    od_f, n_df = _scan_call("scalar", gdn_in, [(_row_layout(la_f), rows), (_row_layout(jax.nn.sigmoid(b_f)), rows)],
                            s_df, False, b, t, GDN_HEADS, GDN_DK, GDN_DV, **kw)
    od_b, n_db = _scan_call("scalar", gdn_in, [(_row_layout(la_b), rows), (_row_layout(jax.nn.sigmoid(b_b)), rows)],
                            s_db, True, b, t, GDN_HEADS, GDN_DK, GDN_DV, **kw)
    mu_perm = jnp.concatenate([mu[3 * RWKV_C:], jnp.zeros((PROJ_TN - RWKV_LORA_IN,), F32), mu[:3 * RWKV_C]])
    zr = _time_filter(z, L1_LORA, jnp.pad(mu_perm[None, :], ((0, 7), (0, 0))), b, t, "shift")
    o_r = PROJ_TN
    o_wd = 0
    o_ad = o_wd + 2 * RWKV_DECAY_LORA
    o_gd = o_ad + 2 * RWKV_AAA_LORA
    vec = lambda v: (v.reshape(1, RWKV_C),
                     lambda tile: pl.BlockSpec((1, L1_HEADS_PER_STEP * RWKV_N), lambda bi, hi, ti: (0, hi)))
    outs = []
    for rev, (w0, w2, a0, a2), s0, o in ((False, (w0_f, w2_f, a0_f, a2_f), s_wf, 0),
                                       (True, (w0_b, w2_b, a0_b, a2_b), s_wb, 1)):
        lw = _lora(zr[:, o_wd + o * RWKV_DECAY_LORA:o_wd + (o + 1) * RWKV_DECAY_LORA], w2, w0,
                   pre="tanh", post="rwkv_w")
        asig = _lora(zr[:, o_ad + o * RWKV_AAA_LORA:o_ad + (o + 1) * RWKV_AAA_LORA], a2, a0, post="sigmoid")
        toks = [(zr, RWKV_N, o_r), (zr, RWKV_N, o_r + RWKV_C), (zr, RWKV_N, o_r + 2 * RWKV_C), (lw, RWKV_N, 0),
                (asig, RWKV_N, 0)]
        outs.append(_scan_call("rwkv", toks, [vec(k_k), vec(k_a), vec(r_k)], s0, rev, b, t, RWKV_HEADS, RWKV_N,
                               RWKV_N, n_tok_out=2, hb=L1_HEADS_PER_STEP))
    (ow_f, bon_f, n_wf), (ow_b, bon_b, n_wb) = outs
    gate = _lora(zr[:, o_gd:o_gd + RWKV_GATE_LORA], g2, None, pre="sigmoid")
    toks = [(od_f, GDN_V, 0), (od_b, GDN_V, 0), (ow_f, RWKV_C, 0), (ow_b, RWKV_C, 0), (bon_f, RWKV_C, 0),
            (bon_b, RWKV_C, 0), (z, GDN_V, L1_GG // GDN_V), (gate, RWKV_C, 0)]
    vecs = [jnp.tile(gdn_norm, GDN_HEADS)[None, :], ln_w[None, :], ln_b[None, :]]
    x = _mix_out(_mix_out_l1_kernel, x, toks, gate_mod, tokens_per_mod, vecs, w_out)
    return x, (n_df, n_db, n_wf, n_wb)


def _permuted_w_in(w, pieces):
    cols = [jnp.zeros((w.shape[0], p), w.dtype) if isinstance(p, int) else w[:, p[0]:p[1]] for p in pieces]
    out = jnp.concatenate(cols, axis=1).astype(BF16)
    assert out.shape[1] % PROJ_TN == 0
    return out


def kernel(x_prompt, x_sample, c, c_ctx, state_l0_gla_fwd, state_l0_gla_bwd, state_l0_ret_fwd, state_l0_ret_bwd, state_l1_gdn_fwd, state_l1_gdn_bwd, state_l1_rwkv_fwd, state_l1_rwkv_bwd, l0_w_mod, l0_b_mod, l0_norm1, l0_norm2, l0_norm3, l0_ffn1_wg, l0_ffn1_wu, l0_ffn1_wd, l0_ffn2_wg, l0_ffn2_wu, l0_ffn2_wd, l0_w_in, l0_w_out, l0_gla_gk_up_fwd, l0_gla_gk_b_fwd, l0_gla_gk_up_bwd, l0_gla_gk_b_bwd, l0_gla_norm, l0_ret_norm, l1_w_mod, l1_b_mod, l1_norm1, l1_norm2, l1_norm3, l1_ffn1_wg, l1_ffn1_wu, l1_ffn1_wd, l1_ffn2_wg, l1_ffn2_wu, l1_ffn2_wd, l1_w_in, l1_w_out, l1_gdn_conv, l1_gdn_A_log_fwd, l1_gdn_dt_bias_fwd, l1_gdn_A_log_bwd, l1_gdn_dt_bias_bwd, l1_gdn_norm, l1_rwkv_mu, l1_rwkv_w0_fwd, l1_rwkv_w2_fwd, l1_rwkv_a0_fwd, l1_rwkv_a2_fwd, l1_rwkv_w0_bwd, l1_rwkv_w2_bwd, l1_rwkv_a0_bwd, l1_rwkv_a2_bwd, l1_rwkv_g2, l1_rwkv_k_k, l1_rwkv_k_a, l1_rwkv_r_k, l1_rwkv_ln_w, l1_rwkv_ln_b, final_norm):
    bp, tp, d = x_prompt.shape
    bs, ts, _ = x_sample.shape
    cond = jnp.concatenate([c_ctx[None, :], c, jnp.zeros((8 - 1 - bs, d), F32)], axis=0)
    common = (
        (l0_w_mod, l0_b_mod, (l0_norm1, l0_norm2, l0_norm3), (l0_ffn1_wg, l0_ffn1_wu, l0_ffn1_wd),
         (l0_ffn2_wg, l0_ffn2_wu, l0_ffn2_wd), l0_w_in, l0_w_out),
        (l1_w_mod, l1_b_mod, (l1_norm1, l1_norm2, l1_norm3), (l1_ffn1_wg, l1_ffn1_wu, l1_ffn1_wd),
         (l1_ffn2_wg, l1_ffn2_wu, l1_ffn2_wd), l1_w_in, l1_w_out),
    )
    caches = (
        (state_l0_gla_fwd, state_l0_gla_bwd, state_l0_ret_fwd, state_l0_ret_bwd),
        (state_l1_gdn_fwd, state_l1_gdn_bwd, state_l1_rwkv_fwd, state_l1_rwkv_bwd),
    )
    l0_pieces = [(0, 2 * GLA_QK + 2 * GLA_V), (L0_END - 2 * RET_QK - 2 * RET_V, L0_END),
                 (2 * GLA_QK + 2 * GLA_V, 2 * GLA_QK + 2 * GLA_V + 2 * GLA_LOWRANK)]
    l0_pieces.append((-L0_END) % PROJ_TN)
    l1_pieces = [(0, GDN_QKV + GDN_V), (GDN_IN + 3 * RWKV_C, GDN_IN + RWKV_IN), (GDN_QKV + GDN_V, GDN_IN),
                 PROJ_TN - RWKV_LORA_IN - 4 * GDN_HEADS, (GDN_IN, GDN_IN + 3 * RWKV_C)]
    w_in_perm = (_permuted_w_in(l0_w_in, l0_pieces), _permuted_w_in(l1_w_in, l1_pieces))
    l0_params = (l0_gla_gk_up_fwd, l0_gla_gk_b_fwd, l0_gla_gk_up_bwd, l0_gla_gk_b_bwd, l0_gla_norm, l0_ret_norm)
    l1_params = (l1_gdn_conv, l1_gdn_A_log_fwd, l1_gdn_dt_bias_fwd, l1_gdn_A_log_bwd, l1_gdn_dt_bias_bwd,
                 l1_gdn_norm, l1_rwkv_mu, l1_rwkv_w0_fwd, l1_rwkv_w2_fwd, l1_rwkv_a0_fwd, l1_rwkv_a2_fwd,
                 l1_rwkv_w0_bwd, l1_rwkv_w2_bwd, l1_rwkv_a0_bwd, l1_rwkv_a2_bwd,
                 l1_rwkv_g2, l1_rwkv_k_k, l1_rwkv_k_a, l1_rwkv_r_k, l1_rwkv_ln_w, l1_rwkv_ln_b)
    bf = lambda w: w.astype(BF16)
    xs = [x_prompt.reshape(bp * tp, d), x_sample.reshape(bs * ts, d)]
    geom = [(bp, tp, bp * tp, slice(0, 1)), (bs, ts, ts, slice(1, 1 + bs))]
    new_states = []
    for layer in range(2):
        w_mod, b_mod, norms, ffn1, ffn2, _, w_out = common[layer]
        m = _adaln(cond, w_mod, b_mod)
        mods = [t_.reshape(8, 1, d) for t_ in jnp.split(m, N_MOD, axis=-1)]
        ffn1 = tuple(bf(w) for w in ffn1)
        ffn2 = tuple(bf(w) for w in ffn2)
        w_out = bf(w_out)
        for gi in range(2):
            b, t, tpm, rows = geom[gi]
            sh1, sc1, g1, sh2, sc2, g2, sh3, sc3, g3 = [mm[rows] for mm in mods]
            x = _ffn(xs[gi], (sh1, sc1, g1), tpm, norms[0], *ffn1, final_norm, False)
            z = _proj_in(x, sh2, sc2, tpm, norms[1], w_in_perm[layer])
            cache = caches[layer] if gi == 1 else None
            if layer == 0:
                x, st = _mixer_gla_ret(x, z, b, t, cache, gi == 1, g2, tpm, w_out, *l0_params)
            else:
                x, st = _mixer_gdn_rwkv(x, z, b, t, cache, g2, tpm, w_out, *l1_params)
            if gi == 0:
                new_states.extend(st)
            xs[gi] = _ffn(x, (sh3, sc3, g3), tpm, norms[2], *ffn2, final_norm, layer == 1)
    return (xs[0].reshape(bp, tp, d), xs[1].reshape(bs, ts, d), *new_states)
```

```python
import functools

import numpy as np
import jax
import jax.numpy as jnp
from jax import lax
from jax.experimental import pallas as pl
from jax.experimental.pallas import tpu as pltpu

F32 = jnp.float32
BF16 = jnp.bfloat16

D_MODEL = 2048
D_FF = 5632
N_MOD = 9
NORM_EPS = 1e-6
GRID_W = 64
ROPE_BASE = 10000.0
CHUNK = 64
SUB = 16
TIME_TILE = 256
L1_HEADS_PER_STEP = 8

GLA_HEADS, GLA_DK, GLA_DV, GLA_LOWRANK, GLA_GATE_NORM = 4, 128, 256, 16, 16.0
GLA_QK, GLA_V = GLA_HEADS * GLA_DK, GLA_HEADS * GLA_DV
RET_HEADS, RET_DK, RET_DV = 4, 128, 256
RET_QK, RET_V = RET_HEADS * RET_DK, RET_HEADS * RET_DV
RET_DECAY_EXP_FWD, RET_DECAY_EXP_BWD = 5.0, 5.5
GDN_HEADS, GDN_DK, GDN_DV = 8, 128, 128
GDN_QK, GDN_V = GDN_HEADS * GDN_DK, GDN_HEADS * GDN_DV
GDN_QKV = 2 * GDN_QK + GDN_V
CONV_K = 5
GDN_IN = GDN_QKV + GDN_V + 4 * GDN_HEADS
RWKV_HEADS, RWKV_N = 16, 64
RWKV_C = RWKV_HEADS * RWKV_N
RWKV_DECAY_LORA, RWKV_AAA_LORA, RWKV_GATE_LORA = 64, 64, 128
RWKV_GN_EPS = 64e-5
RWKV_IN = 3 * RWKV_C + 2 * RWKV_DECAY_LORA + 2 * RWKV_AAA_LORA + RWKV_GATE_LORA

L0_GQ, L0_GK, L0_GV, L0_GG = 0, GLA_QK, 2 * GLA_QK, 2 * GLA_QK + GLA_V
L0_RQ = L0_GG + GLA_V
L0_RK, L0_RV = L0_RQ + RET_QK, L0_RQ + 2 * RET_QK
L0_RG = L0_RV + RET_V
L0_GDF = L0_RG + RET_V
L0_GDB = L0_GDF + GLA_LOWRANK
L0_END = L0_GDB + GLA_LOWRANK
L1_GG = GDN_QKV
L1_LORA = GDN_QKV + GDN_V
RWKV_LORA_IN = RWKV_IN - 3 * RWKV_C
L1_SCAL = L1_LORA + RWKV_LORA_IN
PROJ_TN = 512
L1_R = L1_LORA + PROJ_TN
L1_END = L1_R + 3 * RWKV_C
FILTER_BW = 512

VMEM_LIMIT_BYTES = 56 * 1024 * 1024


def _cparams(n_axes):
    return pltpu.CompilerParams(dimension_semantics=("arbitrary",) * n_axes, vmem_limit_bytes=VMEM_LIMIT_BYTES)


def _bdot(a, b):
    return jnp.dot(a.astype(BF16), b.astype(BF16), preferred_element_type=F32)


def _bmm(a, b):
    return lax.dot_general(a.astype(BF16), b.astype(BF16), (((2,), (1,)), ((0,), (0,))),
                           preferred_element_type=F32)


def _bmm_nt(a, b):
    return lax.dot_general(a.astype(BF16), b.astype(BF16), (((2,), (2,)), ((0,), (0,))),
                           preferred_element_type=F32)


def _bmm_tn(a, b):
    return lax.dot_general(a.astype(BF16), b.astype(BF16), (((1,), (1,)), ((0,), (0,))),
                           preferred_element_type=F32)


def _iota2(n, m, axis):
    return lax.broadcasted_iota(jnp.int32, (n, m), axis)


def _split3(x):
    x1 = x.astype(BF16)
    r1 = x - x1.astype(F32)
    x2 = r1.astype(BF16)
    x3 = (r1 - x2.astype(F32)).astype(BF16)
    return x1, x2, x3


def _tile_cumsum(x, rev):
    tc = x.shape[0]
    row = _iota2(tc, tc, 0)
    col = _iota2(tc, tc, 1)
    same = jnp.right_shift(row, 6) == jnp.right_shift(col, 6)
    tri = jnp.where(same & ((row >= col) if not rev else (row <= col)), 1.0, 0.0).astype(BF16)
    x1, x2, x3 = _split3(x)
    d = lambda y: jnp.dot(tri, y, preferred_element_type=F32)
    return d(x1) + (d(x2) + d(x3))


def _lane_group_sum(x, group):
    sh = group.bit_length() - 1
    ones_blk = jnp.where(jnp.right_shift(_iota2(128, 128, 0), sh) == jnp.right_shift(_iota2(128, 128, 1), sh),
                         1.0, 0.0).astype(BF16)
    x1, x2, x3 = _split3(x)
    d = lambda y: jnp.dot(y, ones_blk, preferred_element_type=F32)
    return jnp.concatenate([d(x1[:, j:j + 128]) + (d(x2[:, j:j + 128]) + d(x3[:, j:j + 128]))
                            for j in range(0, x.shape[1], 128)], axis=1)


def _chunk_masks(c, rev):
    row = _iota2(c, c, 0)
    col = _iota2(c, c, 1)
    eye = row == col
    incl = (row >= col) if not rev else (row <= col)
    strict = (row > col) if not rev else (row < col)
    same_blk = jnp.right_shift(row, 4) == jnp.right_shift(col, 4)
    return eye, incl, strict, same_blk


def _silu(x):
    return x * jax.nn.sigmoid(x)


def _softplus(x):
    return jnp.maximum(x, 0.0) + jnp.log1p(jnp.exp(-jnp.abs(x)))


def _split_heads(x, hb, ncs):
    d = x.shape[1] // hb
    return jnp.concatenate([x[:, h * d:(h + 1) * d].reshape(ncs, CHUNK, d) for h in range(hb)], axis=0)


def _gla_phase1(q, k, v, g, rev):
    nb, c, kd = q.shape
    row = _iota2(c, c, 0)
    col = _iota2(c, c, 1)
    ns = c // SUB
    rows = []
    for i in range(ns):
        first = (i == 0) if not rev else (i == ns - 1)
        if first:
            rows.append(jnp.zeros((nb, SUB, c), F32))
            continue
        r0 = i * SUB
        gref = g[:, r0 - 1:r0, :] if not rev else g[:, r0 + SUB:r0 + SUB + 1, :]
        qs = q[:, r0:r0 + SUB, :] * jnp.exp(g[:, r0:r0 + SUB, :] - gref)
        if not rev:
            ks = jnp.concatenate([k[:, :r0] * jnp.exp(gref - g[:, :r0]), jnp.zeros((nb, c - r0, kd), F32)], axis=1)
        else:
            ks = jnp.concatenate([jnp.zeros((nb, r0 + SUB, kd), F32),
                                  k[:, r0 + SUB:] * jnp.exp(gref - g[:, r0 + SUB:])], axis=1)
        rows.append(_bmm_nt(qs, ks))
    att = jnp.concatenate(rows, axis=1)
    blk_r = jnp.right_shift(row, 4)
    blk_c = jnp.right_shift(col, 4)
    dist = (row - col) if not rev else (col - row)
    dist = jnp.where(blk_r == blk_c, dist, -1)
    k2 = k.reshape(nb * c, kd)
    g2 = g.reshape(nb * c, kd)
    for d in range(SUB):
        if d == 0:
            ksh, gsh = k, g
        else:
            sh = d if not rev else nb * c - d
            ksh = pltpu.roll(k2, sh, 0).reshape(nb, c, kd)
            gsh = pltpu.roll(g2, sh, 0).reshape(nb, c, kd)
        e = jnp.exp(jnp.minimum(g - gsh, 0.0))
        band = jnp.sum(q * ksh * e, axis=2, keepdims=True)
        att = jnp.where(dist == d, band, att)
    g_end = g[:, c - 1:c, :] if not rev else g[:, 0:1, :]
    eye_k = _iota2(kd, kd, 0) == _iota2(kd, kd, 1)
    dcol = jnp.sum(jnp.where(eye_k, jnp.exp(g_end), 0.0), axis=2, keepdims=True)
    return q * jnp.exp(g), _bmm(att, v), _bmm_tn(k * jnp.exp(g_end - g), v), dcol


def _gla_kernel(*refs, rev, has_s0, hb, ncs):
    q_ref, k_ref, v_ref, la_ref = refs[:4]
    pos = 4
    s0_ref = None
    if has_s0:
        s0_ref = refs[pos]
        pos += 1
    o_ref, sout_ref, s_scr = refs[pos], refs[pos + 1], refs[pos + 2]
    _init_state(s_scr, s0_ref)
    sp = lambda x: _split_heads(x, hb, ncs)
    r2, y0, s_add, dec = _gla_phase1(sp(q_ref[...] * GLA_DK ** -0.5), sp(k_ref[...]), sp(v_ref[...]),
                                     sp(_tile_cumsum(la_ref[...], rev)), rev)
    s = _phase2(r2, y0, None, 1.0, s_add, dec, s_scr[...], o_ref, hb, ncs, rev)
    s_scr[...] = s

    @pl.when(pl.program_id(2) == pl.num_programs(2) - 1)
    def _():
        sout_ref[0] = s


def _tri_inverse_b(n_mat, eye, same_blk):
    eyef = jnp.where(eye, 1.0, 0.0)
    nd = jnp.where(same_blk, n_mat, 0.0)
    no = n_mat - nd
    p = eyef - nd
    m = nd
    for _ in range(3):
        m = _bmm(m, m)
        p = p + _bmm(p, m)
    mm = _bmm(p, no)
    q = eyef - mm
    q = q + _bmm(q, _bmm(mm, mm))
    return _bmm(q, p)


def _rwkv_phase1(r, lw, g, k, v, a, b, rev):
    nb, c, n = r.shape
    eye, incl, strict, same_blk = _chunk_masks(c, rev)
    eg = jnp.exp(g)
    einv = jnp.exp(-g)
    at = a * jnp.exp(g - lw)
    rt = r * eg
    ar = jnp.concatenate([at, rt], axis=1)
    ab = _bmm_nt(ar, b * einv)
    ak = _bmm_nt(ar, k * einv)
    a_ab = jnp.where(strict, ab[:, :c], 0.0)
    a_rb = jnp.where(incl, ab[:, c:], 0.0)
    a_ak = jnp.where(strict, ak[:, :c], 0.0)
    a_rk = jnp.where(incl, ak[:, c:], 0.0)
    t_inv = _tri_inverse_b(-a_ab, eye, same_blk)
    w = _bmm(t_inv, at)
    u0 = _bmm(t_inv, _bmm(a_ak, v))
    r2 = rt + _bmm(a_rb, w)
    y0 = _bmm(a_rk, v) + _bmm(a_rb, u0)
    g_end = g[:, c - 1:c, :] if not rev else g[:, 0:1, :]
    dec = jnp.exp(g_end - g)
    bd = b * dec
    m = _bmm_tn(bd, w)
    s_add = _bmm_tn(bd, u0) + _bmm_tn(k * dec, v)
    eye_n = _iota2(n, n, 0) == _iota2(n, n, 1)
    dcol = jnp.sum(jnp.where(eye_n, jnp.exp(g_end), 0.0), axis=2, keepdims=True)
    return r2, y0, m, s_add, dcol


def _scalar_phase1(q, k, v, lc_row, beta_row, rev, delta):
    nb, c, _ = q.shape
    eye, incl, strict, same_blk = _chunk_masks(c, rev)
    row = _iota2(c, c, 0)
    col = _iota2(c, c, 1)
    before = (row <= col) if not rev else (row >= col)
    to_col = lambda x: jnp.sum(jnp.where(eye, x, 0.0), axis=2, keepdims=True)
    lc_col = to_col(lc_row)
    g_row = jnp.sum(jnp.where(before, lc_col, 0.0), axis=1, keepdims=True)
    g_col = to_col(g_row)
    rel = jnp.where(incl, jnp.exp(jnp.minimum(g_col - g_row, 0.0)), 0.0)
    eg = jnp.exp(g_col)
    g_end = g_col[:, c - 1:c, :] if not rev else g_col[:, 0:1, :]
    kdec = k * jnp.exp(g_end - g_col)
    if delta:
        beta_col = to_col(beta_row)
        qkk = _bmm_nt(jnp.concatenate([q, k], axis=1), k)
        p = qkk[:, :c] * rel
        n_mat = jnp.where(strict, beta_col * rel * qkk[:, c:], 0.0)
        t_inv = _tri_inverse_b(n_mat, eye, same_blk)
        sol_v = _bmm(t_inv, beta_col * v)
        sol_k = _bmm(t_inv, (beta_col * eg) * k)
        r2 = q * eg - _bmm(p, sol_k)
        y0 = _bmm(p, sol_v)
        m = _bmm_tn(kdec, sol_k)
        s_add = _bmm_tn(kdec, sol_v)
    else:
        p = _bmm_nt(q, k) * rel
        r2 = q * eg
        y0 = _bmm(p, v)
        m = None
        s_add = _bmm_tn(kdec, v)
    return r2, y0, m, s_add, jnp.exp(g_end)


def _phase2(r2, y0, m, m_sign, s_add, dec, s, o_ref, hb, ncs, rev):
    unb = lambda x: x.reshape((hb, ncs) + x.shape[1:])
    r2, y0, s_add, dec = unb(r2), unb(y0), unb(s_add), unb(dec)
    if m is not None:
        m = unb(m)
    for ci in range(ncs):
        cc = (ncs - 1 - ci) if rev else ci
        y = y0[:, cc] + _bmm(r2[:, cc], s)
        o_ref[cc * CHUNK:(cc + 1) * CHUNK, :] = jnp.concatenate([y[h] for h in range(hb)], axis=1)
        s_new = dec[:, cc] * s + s_add[:, cc]
        if m is not None:
            s_new = s_new + m_sign * _bmm(m[:, cc], s)
        s = s_new
    return s


def _init_state(s_scr, s0_ref):
    @pl.when(pl.program_id(2) == 0)
    def _():
        if s0_ref is not None:
            s_scr[...] = s0_ref[0]
        else:
            s_scr[...] = jnp.zeros(s_scr.shape, F32)


def _scalar_kernel(*refs, rev, has_s0, hb, ncs, delta, rotary):
    q_ref, k_ref, v_ref, la_ref, be_ref = refs[:5]
    pos = 5
    cos_ref = sin_ref = s0_ref = None
    if rotary:
        cos_ref, sin_ref = refs[pos], refs[pos + 1]
        pos += 2
    if has_s0:
        s0_ref = refs[pos]
        pos += 1
    o_ref, sout_ref, s_scr = refs[pos], refs[pos + 1], refs[pos + 2]
    _init_state(s_scr, s0_ref)
    nb = hb * ncs
    q = q_ref[...]
    k = k_ref[...]
    dk = q.shape[1] // hb
    if rotary:
        cos2, sin2 = cos_ref[...], sin_ref[...]
        rot = lambda x: jnp.concatenate(
            [x[:, h * dk:(h + 1) * dk] * cos2 + pltpu.roll(x[:, h * dk:(h + 1) * dk], dk // 2, 1) * sin2
             for h in range(hb)], axis=1)
        q, k = rot(q), rot(k)
    q = _split_heads(q, hb, ncs)
    k = _split_heads(k, hb, ncs)
    if delta:
        l2norm = lambda x: x * lax.rsqrt(jnp.sum(x * x, axis=-1, keepdims=True) + NORM_EPS)
        q, k = l2norm(q), l2norm(k)
    q = q * dk ** -0.5
    v = _split_heads(v_ref[...], hb, ncs)
    r2, y0, m, s_add, dec = _scalar_phase1(q, k, v, la_ref[0].reshape(nb, 1, CHUNK),
                                           be_ref[0].reshape(nb, 1, CHUNK), rev, delta)
    s = _phase2(r2, y0, m, -1.0, s_add, dec, s_scr[...], o_ref, hb, ncs, rev)
    s_scr[...] = s

    @pl.when(pl.program_id(2) == pl.num_programs(2) - 1)
    def _():
        sout_ref[0] = s


def _rwkv_kernel(*refs, rev, has_s0, hb, ncs):
    r_ref, kr_ref, v_ref, lw_ref, a_ref, kk_ref, ka_ref, rk_ref = refs[:8]
    pos = 8
    s0_ref = None
    if has_s0:
        s0_ref = refs[pos]
        pos += 1
    o_ref, bonus_ref, sout_ref, s_scr = refs[pos], refs[pos + 1], refs[pos + 2], refs[pos + 3]
    _init_state(s_scr, s0_ref)
    r = r_ref[...]
    kr = kr_ref[...]
    a = a_ref[...]
    kd = kr * (1.0 + (a - 1.0) * ka_ref[...])
    sp = lambda x: _split_heads(x, hb, ncs)
    kk = kr * kk_ref[...]
    kk = kk * lax.rsqrt(_lane_group_sum(kk * kk, RWKV_N) + NORM_EPS)
    v = v_ref[...]
    bonus_ref[...] = _lane_group_sum(r * kd * rk_ref[...], RWKV_N) * v
    lw = lw_ref[...]
    r2, y0, m, s_add, dec = _rwkv_phase1(sp(r), sp(lw), sp(_tile_cumsum(lw, rev)), sp(kd), sp(v), sp(-kk),
                                         sp(kk * a), rev)
    s = _phase2(r2, y0, m, 1.0, s_add, dec, s_scr[...], o_ref, hb, ncs, rev)
    s_scr[...] = s

    @pl.when(pl.program_id(2) == pl.num_programs(2) - 1)
    def _():
        sout_ref[0] = s


def _scan_call(kind, tok_ins, extra_ins, s0, rev, b, t, heads, dk, dv, n_tok_out=1, hb=4, **kw):
    tc = min(TIME_TILE, t)
    nt = t // tc
    ncs = tc // CHUNK
    tile = lambda ti: (nt - 1 - ti) if rev else ti

    def tok_spec(width, off):
        bw = hb * width
        assert off % bw == 0
        cb = off // bw
        return pl.BlockSpec((tc, bw), lambda bi, hi, ti: (bi * nt + tile(ti), cb + hi))

    in_specs = [tok_spec(w, off) for _, w, off in tok_ins] + [mk(tile) for _, mk in extra_ins]
    args = [x for x, _, _ in tok_ins] + [x for x, _ in extra_ins]
    state_spec = pl.BlockSpec((1, hb, dk, dv), lambda bi, hi, ti: (bi, hi, 0, 0))
    if s0 is not None:
        in_specs.append(state_spec)
        args.append(s0)
    body = {"gla": _gla_kernel, "scalar": _scalar_kernel, "rwkv": _rwkv_kernel}[kind]
    outs = pl.pallas_call(
        functools.partial(body, rev=rev, has_s0=s0 is not None, hb=hb, ncs=ncs, **kw),
        grid=(b, heads // hb, nt),
        in_specs=in_specs,
        out_specs=[tok_spec(dv, 0)] * n_tok_out + [state_spec],
        out_shape=[jax.ShapeDtypeStruct((b * t, heads * dv), F32)] * n_tok_out
        + [jax.ShapeDtypeStruct((b, heads, dk, dv), F32)],
        scratch_shapes=[pltpu.VMEM((hb, dk, dv), F32)],
        compiler_params=_cparams(3),
        name=f"scan_{kind}_{'bwd' if rev else 'fwd'}",
    )(*args)
    return outs


def _row_spec(hb, ncs, nt):
    return lambda tile: pl.BlockSpec((1, hb, ncs, 1, CHUNK), lambda bi, hi, ti: (bi, hi, tile(ti), 0, 0))


def _row_layout(x):
    b, h, t = x.shape
    return x.reshape(b, h, t // CHUNK, 1, CHUNK)


def _modulated_norm(x, nw, shift, scale):
    y = x * lax.rsqrt(jnp.mean(x * x, axis=-1, keepdims=True) + NORM_EPS) * nw
    return y * (1.0 + scale) + shift


def _adaln_kernel(c_ref, w_ref, b_ref, o_ref):
    o_ref[...] = _bdot(_silu(c_ref[...]), w_ref[...]) + b_ref[...]


def _adaln(cond, w_mod, b_mod):
    r, d = cond.shape
    n = w_mod.shape[1]
    tn = 1024
    return pl.pallas_call(
        _adaln_kernel,
        grid=(n // tn,),
        in_specs=[pl.BlockSpec((r, d), lambda j: (0, 0)),
                  pl.BlockSpec((d, tn), lambda j: (0, j)),
                  pl.BlockSpec((1, tn), lambda j: (0, j))],
        out_specs=pl.BlockSpec((r, tn), lambda j: (0, j)),
        out_shape=jax.ShapeDtypeStruct((r, n), F32),
        compiler_params=_cparams(1),
        name="adaln",
    )(cond, w_mod, b_mod.reshape(1, n))


def _ffn_kernel(x_ref, sh_ref, sc_ref, g_ref, nw_ref, wg_ref, wu_ref, wd_ref, fn_ref, o_ref, h_scr, *,
                nf, final_norm):
    f = pl.program_id(1)

    @pl.when(f == 0)
    def _():
        h = _modulated_norm(x_ref[...], nw_ref[...], sh_ref[0], sc_ref[0])
        h_scr[...] = h.astype(BF16)
        o_ref[...] = jnp.zeros(o_ref.shape, F32)

    h = h_scr[...]
    gate = jnp.dot(h, wg_ref[...], preferred_element_type=F32)
    up = jnp.dot(h, wu_ref[...], preferred_element_type=F32)
    o_ref[...] += jnp.dot((_silu(gate) * up).astype(BF16), wd_ref[...], preferred_element_type=F32)

    @pl.when(f == nf - 1)
    def _():
        y = x_ref[...] + (0.5 * g_ref[0]) * o_ref[...]
        if final_norm:
            y = y * lax.rsqrt(jnp.mean(y * y, axis=-1, keepdims=True) + NORM_EPS) * fn_ref[...]
        o_ref[...] = y


def _ffn(x, mods, tokens_per_mod, nw, wg, wu, wd, fnorm, final_norm, tm=512, tf=512):
    n, d = x.shape
    shift, scale, gate = mods
    tm = min(tm, tokens_per_mod)
    nf = wg.shape[1] // tf
    mod_spec = pl.BlockSpec((1, 1, d), lambda i, f: ((i * tm) // tokens_per_mod, 0, 0))
    vec_spec = pl.BlockSpec((1, d), lambda i, f: (0, 0))
    return pl.pallas_call(
        functools.partial(_ffn_kernel, nf=nf, final_norm=final_norm),
        grid=(n // tm, nf),
        in_specs=[pl.BlockSpec((tm, d), lambda i, f: (i, 0)), mod_spec, mod_spec, mod_spec, vec_spec,
                  pl.BlockSpec((d, tf), lambda i, f: (0, f)),
                  pl.BlockSpec((d, tf), lambda i, f: (0, f)),
                  pl.BlockSpec((tf, d), lambda i, f: (f, 0)),
                  vec_spec],
        out_specs=pl.BlockSpec((tm, d), lambda i, f: (i, 0)),
        out_shape=jax.ShapeDtypeStruct((n, d), F32),
        scratch_shapes=[pltpu.VMEM((tm, d), BF16)],
        compiler_params=_cparams(2),
        name="ffn",
    )(x, shift, scale, gate, nw.reshape(1, d), wg, wu, wd, fnorm.reshape(1, d))


def _proj_in_kernel(x_ref, sh_ref, sc_ref, nw_ref, w_ref, o_ref, h_scr):
    @pl.when(pl.program_id(1) == 0)
    def _():
        h_scr[...] = _modulated_norm(x_ref[...], nw_ref[...], sh_ref[0], sc_ref[0]).astype(BF16)

    o_ref[...] = jnp.dot(h_scr[...], w_ref[...], preferred_element_type=F32)


def _proj_in(x, shift, scale, tokens_per_mod, nw, w, tm=1024, tn=PROJ_TN):
    n, d = x.shape
    n_out = w.shape[1]
    tm = min(tm, tokens_per_mod)
    mod_spec = pl.BlockSpec((1, 1, d), lambda i, j: ((i * tm) // tokens_per_mod, 0, 0))
    return pl.pallas_call(
        _proj_in_kernel,
        grid=(n // tm, n_out // tn),
        in_specs=[pl.BlockSpec((tm, d), lambda i, j: (i, 0)), mod_spec, mod_spec,
                  pl.BlockSpec((1, d), lambda i, j: (0, 0)),
                  pl.BlockSpec((d, tn), lambda i, j: (0, j))],
        out_specs=pl.BlockSpec((tm, tn), lambda i, j: (i, j)),
        out_shape=jax.ShapeDtypeStruct((n, n_out), F32),
        scratch_shapes=[pltpu.VMEM((tm, d), BF16)],
        compiler_params=_cparams(2),
        name="proj_in",
    )(x, shift, scale, nw.reshape(1, d), w)


def _head_norm_lanes(x, nheads, eps, center):
    d = x.shape[1] // nheads
    outs = []
    for h in range(nheads):
        xh = x[:, h * d:(h + 1) * d]
        if center:
            xh = xh - jnp.mean(xh, axis=-1, keepdims=True)
        outs.append(xh * lax.rsqrt(jnp.mean(xh * xh, axis=-1, keepdims=True) + eps))
    return jnp.concatenate(outs, axis=1)


def _head_norm_half_lanes(x, eps):
    lo = _iota2(1, 128, 1) < 64
    outs = []
    for j in range(x.shape[1] // 128):
        xb = x[:, j * 128:(j + 1) * 128]
        half = lambda y: jnp.where(lo, jnp.sum(jnp.where(lo, y, 0.0), axis=-1, keepdims=True),
                                   jnp.sum(jnp.where(lo, 0.0, y), axis=-1, keepdims=True)) * (1.0 / 64.0)
        xc = xb - half(xb)
        outs.append(xc * lax.rsqrt(half(xc * xc) + eps))
    return jnp.concatenate(outs, axis=1)


def _mix_out_l0_kernel(x_ref, gf_ref, gb_ref, rf_ref, rb_ref, gg_ref, rg_ref, g_ref, gn_ref, rn_ref, w_ref, o_ref):
    og = _head_norm_lanes(gf_ref[...] + gb_ref[...], GLA_HEADS, NORM_EPS, False) * gn_ref[...] * _silu(gg_ref[...])
    orr = _head_norm_lanes(rf_ref[...] + rb_ref[...], RET_HEADS, NORM_EPS, True) * rn_ref[...] * _silu(rg_ref[...])
    u = jnp.concatenate([og, orr], axis=1).astype(BF16)
    o_ref[...] = x_ref[...] + g_ref[0] * jnp.dot(u, w_ref[...], preferred_element_type=F32)


def _mix_out_l1_kernel(x_ref, df_ref, db_ref, wf_ref, wb_ref, bf_ref, bb_ref, gg_ref, gate_ref, g_ref,
                       dn_ref, lnw_ref, lnb_ref, w_ref, o_ref):
    od = _head_norm_lanes(df_ref[...] + db_ref[...], GDN_HEADS, NORM_EPS, False) * dn_ref[...] * _silu(gg_ref[...])
    y = _head_norm_half_lanes(wf_ref[...] + wb_ref[...], RWKV_GN_EPS) * lnw_ref[...] + lnb_ref[...]
    y = (y + bf_ref[...] + bb_ref[...]) * gate_ref[...]
    u = jnp.concatenate([od, y], axis=1).astype(BF16)
    o_ref[...] = x_ref[...] + g_ref[0] * jnp.dot(u, w_ref[...], preferred_element_type=F32)


def _mix_out(kern, x, tok_ins, gate_mod, tokens_per_mod, vecs, w, tm=256):
    n, d = x.shape
    in_specs = [pl.BlockSpec((tm, d), lambda i: (i, 0))]
    in_specs += [pl.BlockSpec((tm, bw), functools.partial(lambda i, cb: (i, cb), cb=cb)) for _, bw, cb in tok_ins]
    in_specs += [pl.BlockSpec((1, 1, d), lambda i: ((i * tm) // tokens_per_mod, 0, 0))]
    in_specs += [pl.BlockSpec(v.shape, lambda i: (0, 0)) for v in vecs]
    in_specs += [pl.BlockSpec(w.shape, lambda i: (0, 0))]
    return pl.pallas_call(
        kern,
        grid=(n // tm,),
        in_specs=in_specs,
        out_specs=pl.BlockSpec((tm, d), lambda i: (i, 0)),
        out_shape=jax.ShapeDtypeStruct((n, d), F32),
        compiler_params=_cparams(1),
        name="mix_out",
    )(x, *[a for a, _, _ in tok_ins], gate_mod, *vecs, w)


def _lora_kernel(x_ref, w_ref, b_ref, o_ref, *, pre, post):
    x = x_ref[...]
    if pre == "tanh":
        x = jnp.tanh(x)
    elif pre == "sigmoid":
        x = jax.nn.sigmoid(x)
    y = _bdot(x, w_ref[...]) + b_ref[...]
    if post == "logsig_gla":
        y = -_softplus(-y) / GLA_GATE_NORM
    elif post == "rwkv_w":
        y = -jnp.exp(-_softplus(-y) - 0.5)
    elif post == "sigmoid":
        y = jax.nn.sigmoid(y)
    o_ref[...] = y


def _lora(x, w, bias, pre=None, post=None, tm=1024):
    n, r = x.shape
    n_out = w.shape[1]
    tm = min(tm, n)
    assert n % tm == 0
    if bias is None:
        bias = jnp.zeros((n_out,), F32)
    return pl.pallas_call(
        functools.partial(_lora_kernel, pre=pre, post=post),
        grid=(n // tm,),
        in_specs=[pl.BlockSpec((tm, r), lambda i: (i, 0)),
                  pl.BlockSpec((r, n_out), lambda i: (0, 0)),
                  pl.BlockSpec((1, n_out), lambda i: (0, 0))],
        out_specs=pl.BlockSpec((tm, n_out), lambda i: (i, 0)),
        out_shape=jax.ShapeDtypeStruct((n, n_out), F32),
        compiler_params=_cparams(1),
        name="lora",
    )(x, w, bias.reshape(1, n_out))


def _rotary_tables(t, dk):
    rows = t // GRID_W
    row = jnp.broadcast_to(jnp.arange(rows, dtype=F32)[:, None], (rows, GRID_W)).reshape(t)
    col = jnp.broadcast_to(jnp.arange(GRID_W, dtype=F32)[None, :], (rows, GRID_W)).reshape(t)
    quarter = dk // 4
    inv = ROPE_BASE ** (-jnp.arange(quarter, dtype=F32) / quarter)
    ang = jnp.concatenate([row[:, None] * inv, col[:, None] * inv], axis=-1)
    cos, sin = jnp.cos(ang), jnp.sin(ang)
    return jnp.concatenate([cos, cos], axis=-1), jnp.concatenate([-sin, sin], axis=-1)


def _retention_log_decay(exp0):
    h = jnp.arange(RET_HEADS, dtype=F32)
    return jnp.log1p(-jnp.power(2.0, -(exp0 + h)))


def _mixer_gla_ret(x, z, b, t, states, latent, gate_mod, tokens_per_mod, w_out, gk_up_f, gk_b_f, gk_up_b, gk_b_b,
                   gla_norm, ret_norm):
    nt = t // min(TIME_TILE, t)
    ncs = min(TIME_TILE, t) // CHUNK
    s_gf, s_gb, s_rf, s_rb = states if states is not None else (None,) * 4
    la_f = _lora(z[:, L0_GDF:L0_GDF + GLA_LOWRANK], gk_up_f, gk_b_f, post="logsig_gla")
    la_b = _lora(z[:, L0_GDB:L0_GDB + GLA_LOWRANK], gk_up_b, gk_b_b, post="logsig_gla")
    gla_in = lambda la: [(z, GLA_DK, L0_GQ), (z, GLA_DK, L0_GK), (z, GLA_DV, L0_GV), (la, GLA_DK, 0)]
    og_f, n_gf = _scan_call("gla", gla_in(la_f), [], s_gf, False, b, t, GLA_HEADS, GLA_DK, GLA_DV)
    og_b, n_gb = _scan_call("gla", gla_in(la_b), [], s_gb, True, b, t, GLA_HEADS, GLA_DK, GLA_DV)
    ones = jnp.ones((b, RET_HEADS, t), F32)
    rows = _row_spec(RET_HEADS, ncs, nt)
    extra = lambda exp0: [(_row_layout(ones * _retention_log_decay(exp0)[None, :, None]), rows),
                          (_row_layout(ones), rows)]
    rot = []
    if latent:
        cos2, sin2 = _rotary_tables(t, RET_DK)
        tab = lambda tile: pl.BlockSpec((min(TIME_TILE, t), RET_DK), lambda bi, hi, ti: (tile(ti), 0))
        rot = [(cos2, tab), (sin2, tab)]
    ret_in = [(z, RET_DK, L0_RQ), (z, RET_DK, L0_RK), (z, RET_DV, L0_RV)]
    kw = dict(delta=False, rotary=latent)
    or_f, n_rf = _scan_call("scalar", ret_in, extra(RET_DECAY_EXP_FWD) + rot, s_rf, False, b, t, RET_HEADS,
                            RET_DK, RET_DV, **kw)
    or_b, n_rb = _scan_call("scalar", ret_in, extra(RET_DECAY_EXP_BWD) + rot, s_rb, True, b, t, RET_HEADS,
                            RET_DK, RET_DV, **kw)
    toks = [(og_f, GLA_V, 0), (og_b, GLA_V, 0), (or_f, RET_V, 0), (or_b, RET_V, 0),
            (z, GLA_V, L0_GG // GLA_V), (z, RET_V, L0_RG // RET_V)]
    vecs = [jnp.tile(gla_norm, GLA_HEADS)[None, :], jnp.tile(ret_norm, RET_HEADS)[None, :]]
    x = _mix_out(_mix_out_l0_kernel, x, toks, gate_mod, tokens_per_mod, vecs, w_out)
    return x, (n_gf, n_gb, n_rf, n_rb)


HALO = 8


def _time_filter_kernel(x_ref, prev_ref, next_ref, p_ref, o_ref, *, mode):
    ti = pl.program_id(1)
    x = x_ref[...]
    tc = x.shape[0]
    prv = jnp.where(ti > 0, prev_ref[...], 0.0)
    nxt = jnp.where(ti < pl.num_programs(1) - 1, next_ref[...], 0.0)
    e = jnp.concatenate([x, nxt, prv], axis=0)
    ne = tc + 2 * HALO
    at = lambda s: pltpu.roll(e, (ne - s) % ne, 0)[:tc]
    p = p_ref[...]
    if mode == "conv":
        acc = x * p[CONV_K // 2:CONV_K // 2 + 1]
        for j in range(CONV_K):
            if j != CONV_K // 2:
                acc = acc + at(j - CONV_K // 2) * p[j:j + 1]
        o_ref[...] = _silu(acc)
    else:
        o_ref[...] = x + p[0:1] * (0.5 * (at(-1) + at(1)) - x)


def _time_filter(z, col0, params, b, t, mode, tc=1024):
    n = z.shape[0]
    width = params.shape[1]
    bw = FILTER_BW
    assert col0 % bw == 0 and width % bw == 0
    tc = min(tc, t)
    nt = t // tc
    cb0 = col0 // bw
    hb_rows = tc // HALO
    return pl.pallas_call(
        functools.partial(_time_filter_kernel, mode=mode),
        grid=(b, nt, width // bw),
        in_specs=[pl.BlockSpec((tc, bw), lambda bi, ti, ci: (bi * nt + ti, cb0 + ci)),
                  pl.BlockSpec((HALO, bw), lambda bi, ti, ci: (jnp.maximum((bi * nt + ti) * hb_rows - 1, 0),
                                                               cb0 + ci)),
                  pl.BlockSpec((HALO, bw), lambda bi, ti, ci: (jnp.minimum((bi * nt + ti + 1) * hb_rows,
                                                                           n // HALO - 1), cb0 + ci)),
                  pl.BlockSpec((8, bw), lambda bi, ti, ci: (0, ci))],
        out_specs=pl.BlockSpec((tc, bw), lambda bi, ti, ci: (bi * nt + ti, ci)),
        out_shape=jax.ShapeDtypeStruct((n, width), F32),
        compiler_params=_cparams(3),
        name=f"time_filter_{mode}",
    )(z, z, z, params)


def _mixer_gdn_rwkv(x, z, b, t, states, gate_mod, tokens_per_mod, w_out, conv_w, A_log_f, dt_bias_f, A_log_b,
                    dt_bias_b, gdn_norm, mu, w0_f, w2_f, a0_f, a2_f, w0_b, w2_b, a0_b, a2_b, g2, k_k, k_a, r_k,
                    ln_w, ln_b):
    n = b * t
    nt = t // min(TIME_TILE, t)
    ncs = min(TIME_TILE, t) // CHUNK
    s_df, s_db, s_wf, s_wb = states if states is not None else (None,) * 4
    qkv = _time_filter(z, 0, jnp.pad(conv_w, ((0, 8 - CONV_K), (0, 0))), b, t, "conv")
    scal = z[:, L1_SCAL:L1_SCAL + 4 * GDN_HEADS].reshape(b, t, 4 * GDN_HEADS)
    a_f, a_b, b_f, b_b = jnp.split(jnp.swapaxes(scal, 1, 2), 4, axis=1)
    la_f = -jnp.exp(A_log_f)[:, None] * jax.nn.softplus(a_f + dt_bias_f[:, None])
    la_b = -jnp.exp(A_log_b)[:, None] * jax.nn.softplus(a_b + dt_bias_b[:, None])
    rows = _row_spec(L1_HEADS_PER_STEP, ncs, nt)
    gdn_in = [(qkv, GDN_DK, 0), (qkv, GDN_DK, GDN_QK), (qkv, GDN_DV, 2 * GDN_QK)]
    kw = dict(delta=True, rotary=False, hb=L1_HEADS_PER_STEP)
    od_f, n_df = _scan_call("scalar", gdn_in, [(_row_layout(la_f), rows), (_row_layout(jax.nn.sigmoid(b_f)), rows)],
                            s_df, False, b, t, GDN_HEADS, GDN_DK, GDN_DV, **kw)
    od_b, n_db = _scan_call("scalar", gdn_in, [(_row_layout(la_b), rows), (_row_layout(jax.nn.sigmoid(b_b)), rows)],
                            s_db, True, b, t, GDN_HEADS, GDN_DK, GDN_DV, **kw)
    mu_perm = jnp.concatenate([mu[3 * RWKV_C:], jnp.zeros((PROJ_TN - RWKV_LORA_IN,), F32), mu[:3 * RWKV_C]])
    zr = _time_filter(z, L1_LORA, jnp.pad(mu_perm[None, :], ((0, 7), (0, 0))), b, t, "shift")
    o_r = PROJ_TN
    o_wd = 0
    o_ad = o_wd + 2 * RWKV_DECAY_LORA
    o_gd = o_ad + 2 * RWKV_AAA_LORA
    vec = lambda v: (v.reshape(1, RWKV_C),
                     lambda tile: pl.BlockSpec((1, L1_HEADS_PER_STEP * RWKV_N), lambda bi, hi, ti: (0, hi)))
    outs = []
    for rev, (w0, w2, a0, a2), s0, o in ((False, (w0_f, w2_f, a0_f, a2_f), s_wf, 0),
                                       (True, (w0_b, w2_b, a0_b, a2_b), s_wb, 1)):
        lw = _lora(zr[:, o_wd + o * RWKV_DECAY_LORA:o_wd + (o + 1) * RWKV_DECAY_LORA], w2, w0,
                   pre="tanh", post="rwkv_w")
        asig = _lora(zr[:, o_ad + o * RWKV_AAA_LORA:o_ad + (o + 1) * RWKV_AAA_LORA], a2, a0, post="sigmoid")
        toks = [(zr, RWKV_N, o_r), (zr, RWKV_N, o_r + RWKV_C), (zr, RWKV_N, o_r + 2 * RWKV_C), (lw, RWKV_N, 0),
                (asig, RWKV_N, 0)]
        outs.append(_scan_call("rwkv", toks, [vec(k_k), vec(k_a), vec(r_k)], s0, rev, b, t, RWKV_HEADS, RWKV_N,
                               RWKV_N, n_tok_out=2, hb=L1_HEADS_PER_STEP))
    (ow_f, bon_f, n_wf), (ow_b, bon_b, n_wb) = outs
    gate = _lora(zr[:, o_gd:o_gd + RWKV_GATE_LORA], g2, None, pre="sigmoid")
    toks = [(od_f, GDN_V, 0), (od_b, GDN_V, 0), (ow_f, RWKV_C, 0), (ow_b, RWKV_C, 0), (bon_f, RWKV_C, 0),
            (bon_b, RWKV_C, 0), (z, GDN_V, L1_GG // GDN_V), (gate, RWKV_C, 0)]
    vecs = [jnp.tile(gdn_norm, GDN_HEADS)[None, :], ln_w[None, :], ln_b[None, :]]
    x = _mix_out(_mix_out_l1_kernel, x, toks, gate_mod, tokens_per_mod, vecs, w_out)
    return x, (n_df, n_db, n_wf, n_wb)


def _permuted_w_in(w, pieces):
    cols = [jnp.zeros((w.shape[0], p), w.dtype) if isinstance(p, int) else w[:, p[0]:p[1]] for p in pieces]
    out = jnp.concatenate(cols, axis=1).astype(BF16)
    assert out.shape[1] % PROJ_TN == 0
    return out


def kernel(x_prompt, x_sample, c, c_ctx, state_l0_gla_fwd, state_l0_gla_bwd, state_l0_ret_fwd, state_l0_ret_bwd, state_l1_gdn_fwd, state_l1_gdn_bwd, state_l1_rwkv_fwd, state_l1_rwkv_bwd, l0_w_mod, l0_b_mod, l0_norm1, l0_norm2, l0_norm3, l0_ffn1_wg, l0_ffn1_wu, l0_ffn1_wd, l0_ffn2_wg, l0_ffn2_wu, l0_ffn2_wd, l0_w_in, l0_w_out, l0_gla_gk_up_fwd, l0_gla_gk_b_fwd, l0_gla_gk_up_bwd, l0_gla_gk_b_bwd, l0_gla_norm, l0_ret_norm, l1_w_mod, l1_b_mod, l1_norm1, l1_norm2, l1_norm3, l1_ffn1_wg, l1_ffn1_wu, l1_ffn1_wd, l1_ffn2_wg, l1_ffn2_wu, l1_ffn2_wd, l1_w_in, l1_w_out, l1_gdn_conv, l1_gdn_A_log_fwd, l1_gdn_dt_bias_fwd, l1_gdn_A_log_bwd, l1_gdn_dt_bias_bwd, l1_gdn_norm, l1_rwkv_mu, l1_rwkv_w0_fwd, l1_rwkv_w2_fwd, l1_rwkv_a0_fwd, l1_rwkv_a2_fwd, l1_rwkv_w0_bwd, l1_rwkv_w2_bwd, l1_rwkv_a0_bwd, l1_rwkv_a2_bwd, l1_rwkv_g2, l1_rwkv_k_k, l1_rwkv_k_a, l1_rwkv_r_k, l1_rwkv_ln_w, l1_rwkv_ln_b, final_norm):
    bp, tp, d = x_prompt.shape
    bs, ts, _ = x_sample.shape
    cond = jnp.concatenate([c_ctx[None, :], c, jnp.zeros((8 - 1 - bs, d), F32)], axis=0)
    common = (
        (l0_w_mod, l0_b_mod, (l0_norm1, l0_norm2, l0_norm3), (l0_ffn1_wg, l0_ffn1_wu, l0_ffn1_wd),
         (l0_ffn2_wg, l0_ffn2_wu, l0_ffn2_wd), l0_w_in, l0_w_out),
        (l1_w_mod, l1_b_mod, (l1_norm1, l1_norm2, l1_norm3), (l1_ffn1_wg, l1_ffn1_wu, l1_ffn1_wd),
         (l1_ffn2_wg, l1_ffn2_wu, l1_ffn2_wd), l1_w_in, l1_w_out),
    )
    caches = (
        (state_l0_gla_fwd, state_l0_gla_bwd, state_l0_ret_fwd, state_l0_ret_bwd),
        (state_l1_gdn_fwd, state_l1_gdn_bwd, state_l1_rwkv_fwd, state_l1_rwkv_bwd),
    )
    l0_pieces = [(0, 2 * GLA_QK + 2 * GLA_V), (L0_END - 2 * RET_QK - 2 * RET_V, L0_END),
                 (2 * GLA_QK + 2 * GLA_V, 2 * GLA_QK + 2 * GLA_V + 2 * GLA_LOWRANK)]
    l0_pieces.append((-L0_END) % PROJ_TN)
    l1_pieces = [(0, GDN_QKV + GDN_V), (GDN_IN + 3 * RWKV_C, GDN_IN + RWKV_IN), (GDN_QKV + GDN_V, GDN_IN),
                 PROJ_TN - RWKV_LORA_IN - 4 * GDN_HEADS, (GDN_IN, GDN_IN + 3 * RWKV_C)]
    w_in_perm = (_permuted_w_in(l0_w_in, l0_pieces), _permuted_w_in(l1_w_in, l1_pieces))
    l0_params = (l0_gla_gk_up_fwd, l0_gla_gk_b_fwd, l0_gla_gk_up_bwd, l0_gla_gk_b_bwd, l0_gla_norm, l0_ret_norm)
    l1_params = (l1_gdn_conv, l1_gdn_A_log_fwd, l1_gdn_dt_bias_fwd, l1_gdn_A_log_bwd, l1_gdn_dt_bias_bwd,
                 l1_gdn_norm, l1_rwkv_mu, l1_rwkv_w0_fwd, l1_rwkv_w2_fwd, l1_rwkv_a0_fwd, l1_rwkv_a2_fwd,
                 l1_rwkv_w0_bwd, l1_rwkv_w2_bwd, l1_rwkv_a0_bwd, l1_rwkv_a2_bwd,
                 l1_rwkv_g2, l1_rwkv_k_k, l1_rwkv_k_a, l1_rwkv_r_k, l1_rwkv_ln_w, l1_rwkv_ln_b)
    bf = lambda w: w.astype(BF16)
    xs = [x_prompt.reshape(bp * tp, d), x_sample.reshape(bs * ts, d)]
    geom = [(bp, tp, bp * tp, slice(0, 1)), (bs, ts, ts, slice(1, 1 + bs))]
    new_states = []
    for layer in range(2):
        w_mod, b_mod, norms, ffn1, ffn2, _, w_out = common[layer]
        m = _adaln(cond, w_mod, b_mod)
        mods = [t_.reshape(8, 1, d) for t_ in jnp.split(m, N_MOD, axis=-1)]
        ffn1 = tuple(bf(w) for w in ffn1)
        ffn2 = tuple(bf(w) for w in ffn2)
        w_out = bf(w_out)
        for gi in range(2):
            b, t, tpm, rows = geom[gi]
            sh1, sc1, g1, sh2, sc2, g2, sh3, sc3, g3 = [mm[rows] for mm in mods]
            x = _ffn(xs[gi], (sh1, sc1, g1), tpm, norms[0], *ffn1, final_norm, False)
            z = _proj_in(x, sh2, sc2, tpm, norms[1], w_in_perm[layer])
            cache = caches[layer] if gi == 1 else None
            if layer == 0:
                x, st = _mixer_gla_ret(x, z, b, t, cache, gi == 1, g2, tpm, w_out, *l0_params)
            else:
                x, st = _mixer_gdn_rwkv(x, z, b, t, cache, g2, tpm, w_out, *l1_params)
            if gi == 0:
                new_states.extend(st)
            xs[gi] = _ffn(x, (sh3, sc3, g3), tpm, norms[2], *ffn2, final_norm, layer == 1)
    return (xs[0].reshape(bp, tp, d), xs[1].reshape(bs, ts, d), *new_states)
```

```python
import functools

import numpy as np
import jax
import jax.numpy as jnp
from jax import lax
from jax.experimental import pallas as pl
from jax.experimental.pallas import tpu as pltpu

F32 = jnp.float32
BF16 = jnp.bfloat16

D_MODEL = 2048
D_FF = 5632
N_MOD = 9
NORM_EPS = 1e-6
GRID_W = 64
ROPE_BASE = 10000.0
CHUNK = 64
SUB = 16
TIME_TILE = 256
L1_HEADS_PER_STEP = 8

GLA_HEADS, GLA_DK, GLA_DV, GLA_LOWRANK, GLA_GATE_NORM = 4, 128, 256, 16, 16.0
GLA_QK, GLA_V = GLA_HEADS * GLA_DK, GLA_HEADS * GLA_DV
RET_HEADS, RET_DK, RET_DV = 4, 128, 256
RET_QK, RET_V = RET_HEADS * RET_DK, RET_HEADS * RET_DV
RET_DECAY_EXP_FWD, RET_DECAY_EXP_BWD = 5.0, 5.5
GDN_HEADS, GDN_DK, GDN_DV = 8, 128, 128
GDN_QK, GDN_V = GDN_HEADS * GDN_DK, GDN_HEADS * GDN_DV
GDN_QKV = 2 * GDN_QK + GDN_V
CONV_K = 5
GDN_IN = GDN_QKV + GDN_V + 4 * GDN_HEADS
RWKV_HEADS, RWKV_N = 16, 64
RWKV_C = RWKV_HEADS * RWKV_N
RWKV_DECAY_LORA, RWKV_AAA_LORA, RWKV_GATE_LORA = 64, 64, 128
RWKV_GN_EPS = 64e-5
RWKV_IN = 3 * RWKV_C + 2 * RWKV_DECAY_LORA + 2 * RWKV_AAA_LORA + RWKV_GATE_LORA

L0_GQ, L0_GK, L0_GV, L0_GG = 0, GLA_QK, 2 * GLA_QK, 2 * GLA_QK + GLA_V
L0_RQ = L0_GG + GLA_V
L0_RK, L0_RV = L0_RQ + RET_QK, L0_RQ + 2 * RET_QK
L0_RG = L0_RV + RET_V
L0_GDF = L0_RG + RET_V
L0_GDB = L0_GDF + GLA_LOWRANK
L0_END = L0_GDB + GLA_LOWRANK
L1_GG = GDN_QKV
L1_LORA = GDN_QKV + GDN_V
RWKV_LORA_IN = RWKV_IN - 3 * RWKV_C
L1_SCAL = L1_LORA + RWKV_LORA_IN
PROJ_TN = 512
L1_R = L1_LORA + PROJ_TN
L1_END = L1_R + 3 * RWKV_C
FILTER_BW = 512

VMEM_LIMIT_BYTES = 56 * 1024 * 1024


def _cparams(n_axes):
    return pltpu.CompilerParams(dimension_semantics=("arbitrary",) * n_axes, vmem_limit_bytes=VMEM_LIMIT_BYTES)


def _bdot(a, b):
    return jnp.dot(a.astype(BF16), b.astype(BF16), preferred_element_type=F32)


def _bmm(a, b):
    return lax.dot_general(a.astype(BF16), b.astype(BF16), (((2,), (1,)), ((0,), (0,))),
                           preferred_element_type=F32)


def _bmm_nt(a, b):
    return lax.dot_general(a.astype(BF16), b.astype(BF16), (((2,), (2,)), ((0,), (0,))),
                           preferred_element_type=F32)


def _bmm_tn(a, b):
    return lax.dot_general(a.astype(BF16), b.astype(BF16), (((1,), (1,)), ((0,), (0,))),
                           preferred_element_type=F32)


def _iota2(n, m, axis):
    return lax.broadcasted_iota(jnp.int32, (n, m), axis)


def _split3(x):
    x1 = x.astype(BF16)
    r1 = x - x1.astype(F32)
    x2 = r1.astype(BF16)
    x3 = (r1 - x2.astype(F32)).astype(BF16)
    return x1, x2, x3


def _tile_cumsum(x, rev):
    tc = x.shape[0]
    row = _iota2(tc, tc, 0)
    col = _iota2(tc, tc, 1)
    same = jnp.right_shift(row, 6) == jnp.right_shift(col, 6)
    tri = jnp.where(same & ((row >= col) if not rev else (row <= col)), 1.0, 0.0).astype(BF16)
    x1, x2, x3 = _split3(x)
    d = lambda y: jnp.dot(tri, y, preferred_element_type=F32)
    return d(x1) + (d(x2) + d(x3))


def _lane_group_sum(x, group):
    sh = group.bit_length() - 1
    ones_blk = jnp.where(jnp.right_shift(_iota2(128, 128, 0), sh) == jnp.right_shift(_iota2(128, 128, 1), sh),
                         1.0, 0.0).astype(BF16)
    x1, x2, x3 = _split3(x)
    d = lambda y: jnp.dot(y, ones_blk, preferred_element_type=F32)
    return jnp.concatenate([d(x1[:, j:j + 128]) + (d(x2[:, j:j + 128]) + d(x3[:, j:j + 128]))
                            for j in range(0, x.shape[1], 128)], axis=1)


def _chunk_masks(c, rev):
    row = _iota2(c, c, 0)
    col = _iota2(c, c, 1)
    eye = row == col
    incl = (row >= col) if not rev else (row <= col)
    strict = (row > col) if not rev else (row < col)
    same_blk = jnp.right_shift(row, 4) == jnp.right_shift(col, 4)
    return eye, incl, strict, same_blk


def _silu(x):
    return x * jax.nn.sigmoid(x)


def _softplus(x):
    return jnp.maximum(x, 0.0) + jnp.log1p(jnp.exp(-jnp.abs(x)))


def _split_heads(x, hb, ncs):
    d = x.shape[1] // hb
    return jnp.concatenate([x[:, h * d:(h + 1) * d].reshape(ncs, CHUNK, d) for h in range(hb)], axis=0)


def _gla_phase1(q, k, v, g, rev):
    nb, c, kd = q.shape
    row = _iota2(c, c, 0)
    col = _iota2(c, c, 1)
    ns = c // SUB
    rows = []
    for i in range(ns):
        first = (i == 0) if not rev else (i == ns - 1)
        if first:
            rows.append(jnp.zeros((nb, SUB, c), F32))
            continue
        r0 = i * SUB
        gref = g[:, r0 - 1:r0, :] if not rev else g[:, r0 + SUB:r0 + SUB + 1, :]
        qs = q[:, r0:r0 + SUB, :] * jnp.exp(g[:, r0:r0 + SUB, :] - gref)
        if not rev:
            ks = jnp.concatenate([k[:, :r0] * jnp.exp(gref - g[:, :r0]), jnp.zeros((nb, c - r0, kd), F32)], axis=1)
        else:
            ks = jnp.concatenate([jnp.zeros((nb, r0 + SUB, kd), F32),
                                  k[:, r0 + SUB:] * jnp.exp(gref - g[:, r0 + SUB:])], axis=1)
        rows.append(_bmm_nt(qs, ks))
    att = jnp.concatenate(rows, axis=1)
    blk_r = jnp.right_shift(row, 4)
    blk_c = jnp.right_shift(col, 4)
    dist = (row - col) if not rev else (col - row)
    dist = jnp.where(blk_r == blk_c, dist, -1)
    k2 = k.reshape(nb * c, kd)
    g2 = g.reshape(nb * c, kd)
    for d in range(SUB):
        if d == 0:
            ksh, gsh = k, g
        else:
            sh = d if not rev else nb * c - d
            ksh = pltpu.roll(k2, sh, 0).reshape(nb, c, kd)
            gsh = pltpu.roll(g2, sh, 0).reshape(nb, c, kd)
        e = jnp.exp(jnp.minimum(g - gsh, 0.0))
        band = jnp.sum(q * ksh * e, axis=2, keepdims=True)
        att = jnp.where(dist == d, band, att)
    g_end = g[:, c - 1:c, :] if not rev else g[:, 0:1, :]
    eye_k = _iota2(kd, kd, 0) == _iota2(kd, kd, 1)
    dcol = jnp.sum(jnp.where(eye_k, jnp.exp(g_end), 0.0), axis=2, keepdims=True)
    return q * jnp.exp(g), _bmm(att, v), _bmm_tn(k * jnp.exp(g_end - g), v), dcol


def _gla_kernel(*refs, rev, has_s0, hb, ncs):
    q_ref, k_ref, v_ref, gd_ref, w_ref, b_ref = refs[:6]
    pos = 6
    s0_ref = None
    if has_s0:
        s0_ref = refs[pos]
        pos += 1
    o_ref, sout_ref, s_scr = refs[pos], refs[pos + 1], refs[pos + 2]
    _init_state(s_scr, s0_ref)
    sp = lambda x: _split_heads(x, hb, ncs)
    la = -_softplus(-(_bdot(gd_ref[...], w_ref[...]) + b_ref[...])) * (1.0 / GLA_GATE_NORM)
    r2, y0, s_add, dec = _gla_phase1(sp(q_ref[...] * GLA_DK ** -0.5), sp(k_ref[...]), sp(v_ref[...]),
                                     sp(_tile_cumsum(la, rev)), rev)
    s = _phase2(r2, y0, None, 1.0, s_add, dec, s_scr[...], o_ref, hb, ncs, rev)
    s_scr[...] = s

    @pl.when(pl.program_id(2) == pl.num_programs(2) - 1)
    def _():
        sout_ref[0] = s


def _tri_inverse_b(n_mat, eye, same_blk):
    eyef = jnp.where(eye, 1.0, 0.0)
    nd = jnp.where(same_blk, n_mat, 0.0)
    no = n_mat - nd
    p = eyef - nd
    m = nd
    for _ in range(3):
        m = _bmm(m, m)
        p = p + _bmm(p, m)
    mm = _bmm(p, no)
    q = eyef - mm
    q = q + _bmm(q, _bmm(mm, mm))
    return _bmm(q, p)


def _rwkv_phase1(r, lw, g, k, v, a, b, rev):
    nb, c, n = r.shape
    eye, incl, strict, same_blk = _chunk_masks(c, rev)
    eg = jnp.exp(g)
    einv = jnp.exp(-g)
    at = a * jnp.exp(g - lw)
    rt = r * eg
    ar = jnp.concatenate([at, rt], axis=1)
    ab = _bmm_nt(ar, b * einv)
    ak = _bmm_nt(ar, k * einv)
    a_ab = jnp.where(strict, ab[:, :c], 0.0)
    a_rb = jnp.where(incl, ab[:, c:], 0.0)
    a_ak = jnp.where(strict, ak[:, :c], 0.0)
    a_rk = jnp.where(incl, ak[:, c:], 0.0)
    t_inv = _tri_inverse_b(-a_ab, eye, same_blk)
    w = _bmm(t_inv, at)
    u0 = _bmm(t_inv, _bmm(a_ak, v))
    r2 = rt + _bmm(a_rb, w)
    y0 = _bmm(a_rk, v) + _bmm(a_rb, u0)
    g_end = g[:, c - 1:c, :] if not rev else g[:, 0:1, :]
    dec = jnp.exp(g_end - g)
    bd = b * dec
    m = _bmm_tn(bd, w)
    s_add = _bmm_tn(bd, u0) + _bmm_tn(k * dec, v)
    eye_n = _iota2(n, n, 0) == _iota2(n, n, 1)
    dcol = jnp.sum(jnp.where(eye_n, jnp.exp(g_end), 0.0), axis=2, keepdims=True)
    return r2, y0, m, s_add, dcol


def _scalar_phase1(q, k, v, lc_row, beta_row, rev, delta):
    nb, c, _ = q.shape
    eye, incl, strict, same_blk = _chunk_masks(c, rev)
    row = _iota2(c, c, 0)
    col = _iota2(c, c, 1)
    before = (row <= col) if not rev else (row >= col)
    to_col = lambda x: jnp.sum(jnp.where(eye, x, 0.0), axis=2, keepdims=True)
    lc_col = to_col(lc_row)
    g_row = jnp.sum(jnp.where(before, lc_col, 0.0), axis=1, keepdims=True)
    g_col = to_col(g_row)
    rel = jnp.where(incl, jnp.exp(jnp.minimum(g_col - g_row, 0.0)), 0.0)
    eg = jnp.exp(g_col)
    g_end = g_col[:, c - 1:c, :] if not rev else g_col[:, 0:1, :]
    kdec = k * jnp.exp(g_end - g_col)
    if delta:
        beta_col = to_col(beta_row)
        qkk = _bmm_nt(jnp.concatenate([q, k], axis=1), k)
        p = qkk[:, :c] * rel
        n_mat = jnp.where(strict, beta_col * rel * qkk[:, c:], 0.0)
        t_inv = _tri_inverse_b(n_mat, eye, same_blk)
        sol_v = _bmm(t_inv, beta_col * v)
        sol_k = _bmm(t_inv, (beta_col * eg) * k)
        r2 = q * eg - _bmm(p, sol_k)
        y0 = _bmm(p, sol_v)
        m = _bmm_tn(kdec, sol_k)
        s_add = _bmm_tn(kdec, sol_v)
    else:
        p = _bmm_nt(q, k) * rel
        r2 = q * eg
        y0 = _bmm(p, v)
        m = None
        s_add = _bmm_tn(kdec, v)
    return r2, y0, m, s_add, jnp.exp(g_end)


def _phase2(r2, y0, m, m_sign, s_add, dec, s, o_ref, hb, ncs, rev):
    unb = lambda x: x.reshape((hb, ncs) + x.shape[1:])
    r2, y0, s_add, dec = unb(r2), unb(y0), unb(s_add), unb(dec)
    if m is not None:
        m = unb(m)
    for ci in range(ncs):
        cc = (ncs - 1 - ci) if rev else ci
        y = y0[:, cc] + _bmm(r2[:, cc], s)
        o_ref[cc * CHUNK:(cc + 1) * CHUNK, :] = jnp.concatenate([y[h] for h in range(hb)], axis=1)
        s_new = dec[:, cc] * s + s_add[:, cc]
        if m is not None:
            s_new = s_new + m_sign * _bmm(m[:, cc], s)
        s = s_new
    return s


def _init_state(s_scr, s0_ref):
    @pl.when(pl.program_id(2) == 0)
    def _():
        if s0_ref is not None:
            s_scr[...] = s0_ref[0]
        else:
            s_scr[...] = jnp.zeros(s_scr.shape, F32)


def _scalar_kernel(*refs, rev, has_s0, hb, ncs, delta, rotary):
    q_ref, k_ref, v_ref, la_ref, be_ref = refs[:5]
    pos = 5
    cos_ref = sin_ref = s0_ref = None
    if rotary:
        cos_ref, sin_ref = refs[pos], refs[pos + 1]
        pos += 2
    if has_s0:
        s0_ref = refs[pos]
        pos += 1
    o_ref, sout_ref, s_scr = refs[pos], refs[pos + 1], refs[pos + 2]
    _init_state(s_scr, s0_ref)
    nb = hb * ncs
    q = q_ref[...]
    k = k_ref[...]
    dk = q.shape[1] // hb
    if rotary:
        cos2, sin2 = cos_ref[...], sin_ref[...]
        rot = lambda x: jnp.concatenate(
            [x[:, h * dk:(h + 1) * dk] * cos2 + pltpu.roll(x[:, h * dk:(h + 1) * dk], dk // 2, 1) * sin2
             for h in range(hb)], axis=1)
        q, k = rot(q), rot(k)
    q = _split_heads(q, hb, ncs)
    k = _split_heads(k, hb, ncs)
    if delta:
        l2norm = lambda x: x * lax.rsqrt(jnp.sum(x * x, axis=-1, keepdims=True) + NORM_EPS)
        q, k = l2norm(q), l2norm(k)
    q = q * dk ** -0.5
    v = _split_heads(v_ref[...], hb, ncs)
    r2, y0, m, s_add, dec = _scalar_phase1(q, k, v, la_ref[0].reshape(nb, 1, CHUNK),
                                           be_ref[0].reshape(nb, 1, CHUNK), rev, delta)
    s = _phase2(r2, y0, m, -1.0, s_add, dec, s_scr[...], o_ref, hb, ncs, rev)
    s_scr[...] = s

    @pl.when(pl.program_id(2) == pl.num_programs(2) - 1)
    def _():
        sout_ref[0] = s


def _rwkv_kernel(*refs, rev, has_s0, hb, ncs):
    r_ref, kr_ref, v_ref, wd_ref, ad_ref, w2_ref, w0_ref, a2_ref, a0_ref, kk_ref, ka_ref, rk_ref = refs[:12]
    pos = 12
    s0_ref = None
    if has_s0:
        s0_ref = refs[pos]
        pos += 1
    o_ref, bonus_ref, sout_ref, s_scr = refs[pos], refs[pos + 1], refs[pos + 2], refs[pos + 3]
    _init_state(s_scr, s0_ref)
    r = r_ref[...]
    kr = kr_ref[...]
    lw = -jnp.exp(-_softplus(-(_bdot(jnp.tanh(wd_ref[...]), w2_ref[...]) + w0_ref[...])) - 0.5)
    a = jax.nn.sigmoid(_bdot(ad_ref[...], a2_ref[...]) + a0_ref[...])
    kd = kr * (1.0 + (a - 1.0) * ka_ref[...])
    sp = lambda x: _split_heads(x, hb, ncs)
    kk = kr * kk_ref[...]
    kk = kk * lax.rsqrt(_lane_group_sum(kk * kk, RWKV_N) + NORM_EPS)
    v = v_ref[...]
    bonus_ref[...] = _lane_group_sum(r * kd * rk_ref[...], RWKV_N) * v
    r2, y0, m, s_add, dec = _rwkv_phase1(sp(r), sp(lw), sp(_tile_cumsum(lw, rev)), sp(kd), sp(v), sp(-kk),
                                         sp(kk * a), rev)
    s = _phase2(r2, y0, m, 1.0, s_add, dec, s_scr[...], o_ref, hb, ncs, rev)
    s_scr[...] = s

    @pl.when(pl.program_id(2) == pl.num_programs(2) - 1)
    def _():
        sout_ref[0] = s


def _scan_call(kind, tok_ins, extra_ins, s0, rev, b, t, heads, dk, dv, n_tok_out=1, hb=4, **kw):
    tc = min(TIME_TILE, t)
    nt = t // tc
    ncs = tc // CHUNK
    tile = lambda ti: (nt - 1 - ti) if rev else ti

    def tok_spec(width, off):
        bw = hb * width
        assert off % bw == 0
        cb = off // bw
        return pl.BlockSpec((tc, bw), lambda bi, hi, ti: (bi * nt + tile(ti), cb + hi))

    in_specs = [tok_spec(w, off) for _, w, off in tok_ins] + [mk(tile) for _, mk in extra_ins]
    args = [x for x, _, _ in tok_ins] + [x for x, _ in extra_ins]
    state_spec = pl.BlockSpec((1, hb, dk, dv), lambda bi, hi, ti: (bi, hi, 0, 0))
    if s0 is not None:
        in_specs.append(state_spec)
        args.append(s0)
    body = {"gla": _gla_kernel, "scalar": _scalar_kernel, "rwkv": _rwkv_kernel}[kind]
    outs = pl.pallas_call(
        functools.partial(body, rev=rev, has_s0=s0 is not None, hb=hb, ncs=ncs, **kw),
        grid=(b, heads // hb, nt),
        in_specs=in_specs,
        out_specs=[tok_spec(dv, 0)] * n_tok_out + [state_spec],
        out_shape=[jax.ShapeDtypeStruct((b * t, heads * dv), F32)] * n_tok_out
        + [jax.ShapeDtypeStruct((b, heads, dk, dv), F32)],
        scratch_shapes=[pltpu.VMEM((hb, dk, dv), F32)],
        compiler_params=_cparams(3),
        name=f"scan_{kind}_{'bwd' if rev else 'fwd'}",
    )(*args)
    return outs


def _row_spec(hb, ncs, nt):
    return lambda tile: pl.BlockSpec((1, hb, ncs, 1, CHUNK), lambda bi, hi, ti: (bi, hi, tile(ti), 0, 0))


def _row_layout(x):
    b, h, t = x.shape
    return x.reshape(b, h, t // CHUNK, 1, CHUNK)


def _modulated_norm(x, nw, shift, scale):
    y = x * lax.rsqrt(jnp.mean(x * x, axis=-1, keepdims=True) + NORM_EPS) * nw
    return y * (1.0 + scale) + shift


def _adaln_kernel(c_ref, w_ref, b_ref, o_ref):
    o_ref[...] = _bdot(_silu(c_ref[...]), w_ref[...]) + b_ref[...]


def _adaln(cond, w_mod, b_mod):
    r, d = cond.shape
    n = w_mod.shape[1]
    tn = 1024
    return pl.pallas_call(
        _adaln_kernel,
        grid=(n // tn,),
        in_specs=[pl.BlockSpec((r, d), lambda j: (0, 0)),
                  pl.BlockSpec((d, tn), lambda j: (0, j)),
                  pl.BlockSpec((1, tn), lambda j: (0, j))],
        out_specs=pl.BlockSpec((r, tn), lambda j: (0, j)),
        out_shape=jax.ShapeDtypeStruct((r, n), F32),
        compiler_params=_cparams(1),
        name="adaln",
    )(cond, w_mod, b_mod.reshape(1, n))


def _ffn_kernel(x_ref, sh_ref, sc_ref, g_ref, nw_ref, wg_ref, wu_ref, wd_ref, fn_ref, o_ref, h_scr, *,
                nf, final_norm):
    f = pl.program_id(1)

    @pl.when(f == 0)
    def _():
        h = _modulated_norm(x_ref[...], nw_ref[...], sh_ref[0], sc_ref[0])
        h_scr[...] = h.astype(BF16)
        o_ref[...] = jnp.zeros(o_ref.shape, F32)

    h = h_scr[...]
    gate = jnp.dot(h, wg_ref[...], preferred_element_type=F32)
    up = jnp.dot(h, wu_ref[...], preferred_element_type=F32)
    o_ref[...] += jnp.dot((_silu(gate) * up).astype(BF16), wd_ref[...], preferred_element_type=F32)

    @pl.when(f == nf - 1)
    def _():
        y = x_ref[...] + (0.5 * g_ref[0]) * o_ref[...]
        if final_norm:
            y = y * lax.rsqrt(jnp.mean(y * y, axis=-1, keepdims=True) + NORM_EPS) * fn_ref[...]
        o_ref[...] = y


def _ffn(x, mods, tokens_per_mod, nw, wg, wu, wd, fnorm, final_norm, tm=512, tf=512):
    n, d = x.shape
    shift, scale, gate = mods
    tm = min(tm, tokens_per_mod)
    nf = wg.shape[1] // tf
    mod_spec = pl.BlockSpec((1, 1, d), lambda i, f: ((i * tm) // tokens_per_mod, 0, 0))
    vec_spec = pl.BlockSpec((1, d), lambda i, f: (0, 0))
    return pl.pallas_call(
        functools.partial(_ffn_kernel, nf=nf, final_norm=final_norm),
        grid=(n // tm, nf),
        in_specs=[pl.BlockSpec((tm, d), lambda i, f: (i, 0)), mod_spec, mod_spec, mod_spec, vec_spec,
                  pl.BlockSpec((d, tf), lambda i, f: (0, f)),
                  pl.BlockSpec((d, tf), lambda i, f: (0, f)),
                  pl.BlockSpec((tf, d), lambda i, f: (f, 0)),
                  vec_spec],
        out_specs=pl.BlockSpec((tm, d), lambda i, f: (i, 0)),
        out_shape=jax.ShapeDtypeStruct((n, d), F32),
        scratch_shapes=[pltpu.VMEM((tm, d), BF16)],
        compiler_params=_cparams(2),
        name="ffn",
    )(x, shift, scale, gate, nw.reshape(1, d), wg, wu, wd, fnorm.reshape(1, d))


def _proj_in_kernel(x_ref, sh_ref, sc_ref, nw_ref, w_ref, o_ref, h_scr):
    @pl.when(pl.program_id(1) == 0)
    def _():
        h_scr[...] = _modulated_norm(x_ref[...], nw_ref[...], sh_ref[0], sc_ref[0]).astype(BF16)

    o_ref[...] = jnp.dot(h_scr[...], w_ref[...], preferred_element_type=F32)


def _proj_in(x, shift, scale, tokens_per_mod, nw, w, tm=1024, tn=PROJ_TN):
    n, d = x.shape
    n_out = w.shape[1]
    tm = min(tm, tokens_per_mod)
    mod_spec = pl.BlockSpec((1, 1, d), lambda i, j: ((i * tm) // tokens_per_mod, 0, 0))
    return pl.pallas_call(
        _proj_in_kernel,
        grid=(n // tm, n_out // tn),
        in_specs=[pl.BlockSpec((tm, d), lambda i, j: (i, 0)), mod_spec, mod_spec,
                  pl.BlockSpec((1, d), lambda i, j: (0, 0)),
                  pl.BlockSpec((d, tn), lambda i, j: (0, j))],
        out_specs=pl.BlockSpec((tm, tn), lambda i, j: (i, j)),
        out_shape=jax.ShapeDtypeStruct((n, n_out), F32),
        scratch_shapes=[pltpu.VMEM((tm, d), BF16)],
        compiler_params=_cparams(2),
        name="proj_in",
    )(x, shift, scale, nw.reshape(1, d), w)


def _head_norm_lanes(x, nheads, eps, center):
    d = x.shape[1] // nheads
    outs = []
    for h in range(nheads):
        xh = x[:, h * d:(h + 1) * d]
        if center:
            xh = xh - jnp.mean(xh, axis=-1, keepdims=True)
        outs.append(xh * lax.rsqrt(jnp.mean(xh * xh, axis=-1, keepdims=True) + eps))
    return jnp.concatenate(outs, axis=1)


def _head_norm_half_lanes(x, eps):
    lo = _iota2(1, 128, 1) < 64
    outs = []
    for j in range(x.shape[1] // 128):
        xb = x[:, j * 128:(j + 1) * 128]
        half = lambda y: jnp.where(lo, jnp.sum(jnp.where(lo, y, 0.0), axis=-1, keepdims=True),
                                   jnp.sum(jnp.where(lo, 0.0, y), axis=-1, keepdims=True)) * (1.0 / 64.0)
        xc = xb - half(xb)
        outs.append(xc * lax.rsqrt(half(xc * xc) + eps))
    return jnp.concatenate(outs, axis=1)


def _mix_out_l0_kernel(x_ref, gf_ref, gb_ref, rf_ref, rb_ref, gg_ref, rg_ref, g_ref, gn_ref, rn_ref, w_ref, o_ref):
    og = _head_norm_lanes(gf_ref[...] + gb_ref[...], GLA_HEADS, NORM_EPS, False) * gn_ref[...] * _silu(gg_ref[...])
    orr = _head_norm_lanes(rf_ref[...] + rb_ref[...], RET_HEADS, NORM_EPS, True) * rn_ref[...] * _silu(rg_ref[...])
    u = jnp.concatenate([og, orr], axis=1).astype(BF16)
    o_ref[...] = x_ref[...] + g_ref[0] * jnp.dot(u, w_ref[...], preferred_element_type=F32)


def _mix_out_l1_kernel(x_ref, df_ref, db_ref, wf_ref, wb_ref, bf_ref, bb_ref, gg_ref, gd_ref, g_ref,
                       dn_ref, lnw_ref, lnb_ref, g2_ref, w_ref, o_ref):
    od = _head_norm_lanes(df_ref[...] + db_ref[...], GDN_HEADS, NORM_EPS, False) * dn_ref[...] * _silu(gg_ref[...])
    y = _head_norm_half_lanes(wf_ref[...] + wb_ref[...], RWKV_GN_EPS) * lnw_ref[...] + lnb_ref[...]
    y = (y + bf_ref[...] + bb_ref[...]) * _bdot(jax.nn.sigmoid(gd_ref[...]), g2_ref[...])
    u = jnp.concatenate([od, y], axis=1).astype(BF16)
    o_ref[...] = x_ref[...] + g_ref[0] * jnp.dot(u, w_ref[...], preferred_element_type=F32)


def _mix_out(kern, x, tok_ins, gate_mod, tokens_per_mod, vecs, w, tm=256):
    n, d = x.shape
    in_specs = [pl.BlockSpec((tm, d), lambda i: (i, 0))]
    in_specs += [pl.BlockSpec((tm, bw), functools.partial(lambda i, cb: (i, cb), cb=cb)) for _, bw, cb in tok_ins]
    in_specs += [pl.BlockSpec((1, 1, d), lambda i: ((i * tm) // tokens_per_mod, 0, 0))]
    in_specs += [pl.BlockSpec(v.shape, lambda i: (0, 0)) for v in vecs]
    in_specs += [pl.BlockSpec(w.shape, lambda i: (0, 0))]
    return pl.pallas_call(
        kern,
        grid=(n // tm,),
        in_specs=in_specs,
        out_specs=pl.BlockSpec((tm, d), lambda i: (i, 0)),
        out_shape=jax.ShapeDtypeStruct((n, d), F32),
        compiler_params=_cparams(1),
        name="mix_out",
    )(x, *[a for a, _, _ in tok_ins], gate_mod, *vecs, w)


def _rotary_tables(t, dk):
    rows = t // GRID_W
    row = jnp.broadcast_to(jnp.arange(rows, dtype=F32)[:, None], (rows, GRID_W)).reshape(t)
    col = jnp.broadcast_to(jnp.arange(GRID_W, dtype=F32)[None, :], (rows, GRID_W)).reshape(t)
    quarter = dk // 4
    inv = ROPE_BASE ** (-jnp.arange(quarter, dtype=F32) / quarter)
    ang = jnp.concatenate([row[:, None] * inv, col[:, None] * inv], axis=-1)
    cos, sin = jnp.cos(ang), jnp.sin(ang)
    return jnp.concatenate([cos, cos], axis=-1), jnp.concatenate([-sin, sin], axis=-1)


def _retention_log_decay(exp0):
    h = jnp.arange(RET_HEADS, dtype=F32)
    return jnp.log1p(-jnp.power(2.0, -(exp0 + h)))


def _mixer_gla_ret(x, z, b, t, states, latent, gate_mod, tokens_per_mod, w_out, gk_up_f, gk_b_f, gk_up_b, gk_b_b,
                   gla_norm, ret_norm):
    nt = t // min(TIME_TILE, t)
    ncs = min(TIME_TILE, t) // CHUNK
    s_gf, s_gb, s_rf, s_rb = states if states is not None else (None,) * 4
    tct = min(TIME_TILE, t)
    gla_in = [(z, GLA_DK, L0_GQ), (z, GLA_DK, L0_GK), (z, GLA_DV, L0_GV)]
    full = lambda x: (x, lambda tile: pl.BlockSpec(x.shape, lambda bi, hi, ti: (0, 0)))
    gd_in = (z, lambda tile: pl.BlockSpec((tct, 128), lambda bi, hi, ti: (bi * nt + tile(ti), L0_GDF // 128)))
    gate_w = lambda w, lo: full(jnp.pad(w, ((lo, 128 - lo - GLA_LOWRANK), (0, 0))))
    og_f, n_gf = _scan_call("gla", gla_in, [gd_in, gate_w(gk_up_f, 0), full(gk_b_f[None, :])], s_gf, False, b, t,
                            GLA_HEADS, GLA_DK, GLA_DV)
    og_b, n_gb = _scan_call("gla", gla_in, [gd_in, gate_w(gk_up_b, GLA_LOWRANK), full(gk_b_b[None, :])], s_gb, True,
                            b, t, GLA_HEADS, GLA_DK, GLA_DV)
    ones = jnp.ones((b, RET_HEADS, t), F32)
    rows = _row_spec(RET_HEADS, ncs, nt)
    extra = lambda exp0: [(_row_layout(ones * _retention_log_decay(exp0)[None, :, None]), rows),
                          (_row_layout(ones), rows)]
    rot = []
    if latent:
        cos2, sin2 = _rotary_tables(t, RET_DK)
        tab = lambda tile: pl.BlockSpec((min(TIME_TILE, t), RET_DK), lambda bi, hi, ti: (tile(ti), 0))
        rot = [(cos2, tab), (sin2, tab)]
    ret_in = [(z, RET_DK, L0_RQ), (z, RET_DK, L0_RK), (z, RET_DV, L0_RV)]
    kw = dict(delta=False, rotary=latent)
    or_f, n_rf = _scan_call("scalar", ret_in, extra(RET_DECAY_EXP_FWD) + rot, s_rf, False, b, t, RET_HEADS,
                            RET_DK, RET_DV, **kw)
    or_b, n_rb = _scan_call("scalar", ret_in, extra(RET_DECAY_EXP_BWD) + rot, s_rb, True, b, t, RET_HEADS,
                            RET_DK, RET_DV, **kw)
    toks = [(og_f, GLA_V, 0), (og_b, GLA_V, 0), (or_f, RET_V, 0), (or_b, RET_V, 0),
            (z, GLA_V, L0_GG // GLA_V), (z, RET_V, L0_RG // RET_V)]
    vecs = [jnp.tile(gla_norm, GLA_HEADS)[None, :], jnp.tile(ret_norm, RET_HEADS)[None, :]]
    x = _mix_out(_mix_out_l0_kernel, x, toks, gate_mod, tokens_per_mod, vecs, w_out)
    return x, (n_gf, n_gb, n_rf, n_rb)


HALO = 8


def _time_filter_kernel(x_ref, prev_ref, next_ref, p_ref, o_ref, *, mode):
    ti = pl.program_id(1)
    x = x_ref[...]
    tc = x.shape[0]
    prv = jnp.where(ti > 0, prev_ref[...], 0.0)
    nxt = jnp.where(ti < pl.num_programs(1) - 1, next_ref[...], 0.0)
    e = jnp.concatenate([x, nxt, prv], axis=0)
    ne = tc + 2 * HALO
    at = lambda s: pltpu.roll(e, (ne - s) % ne, 0)[:tc]
    p = p_ref[...]
    if mode == "conv":
        acc = x * p[CONV_K // 2:CONV_K // 2 + 1]
        for j in range(CONV_K):
            if j != CONV_K // 2:
                acc = acc + at(j - CONV_K // 2) * p[j:j + 1]
        o_ref[...] = _silu(acc)
    else:
        o_ref[...] = x + p[0:1] * (0.5 * (at(-1) + at(1)) - x)


def _time_filter(z, col0, params, b, t, mode, tc=1024):
    n = z.shape[0]
    width = params.shape[1]
    bw = FILTER_BW
    assert col0 % bw == 0 and width % bw == 0
    tc = min(tc, t)
    nt = t // tc
    cb0 = col0 // bw
    hb_rows = tc // HALO
    return pl.pallas_call(
        functools.partial(_time_filter_kernel, mode=mode),
        grid=(b, nt, width // bw),
        in_specs=[pl.BlockSpec((tc, bw), lambda bi, ti, ci: (bi * nt + ti, cb0 + ci)),
                  pl.BlockSpec((HALO, bw), lambda bi, ti, ci: (jnp.maximum((bi * nt + ti) * hb_rows - 1, 0),
                                                               cb0 + ci)),
                  pl.BlockSpec((HALO, bw), lambda bi, ti, ci: (jnp.minimum((bi * nt + ti + 1) * hb_rows,
                                                                           n // HALO - 1), cb0 + ci)),
                  pl.BlockSpec((8, bw), lambda bi, ti, ci: (0, ci))],
        out_specs=pl.BlockSpec((tc, bw), lambda bi, ti, ci: (bi * nt + ti, ci)),
        out_shape=jax.ShapeDtypeStruct((n, width), F32),
        compiler_params=_cparams(3),
        name=f"time_filter_{mode}",
    )(z, z, z, params)


def _mixer_gdn_rwkv(x, z, b, t, states, gate_mod, tokens_per_mod, w_out, conv_w, A_log_f, dt_bias_f, A_log_b,
                    dt_bias_b, gdn_norm, mu, w0_f, w2_f, a0_f, a2_f, w0_b, w2_b, a0_b, a2_b, g2, k_k, k_a, r_k,
                    ln_w, ln_b):
    n = b * t
    nt = t // min(TIME_TILE, t)
    ncs = min(TIME_TILE, t) // CHUNK
    s_df, s_db, s_wf, s_wb = states if states is not None else (None,) * 4
    qkv = _time_filter(z, 0, jnp.pad(conv_w, ((0, 8 - CONV_K), (0, 0))), b, t, "conv")
    scal = z[:, L1_SCAL:L1_SCAL + 4 * GDN_HEADS].reshape(b, t, 4 * GDN_HEADS)
    a_f, a_b, b_f, b_b = jnp.split(jnp.swapaxes(scal, 1, 2), 4, axis=1)
    la_f = -jnp.exp(A_log_f)[:, None] * jax.nn.softplus(a_f + dt_bias_f[:, None])
    la_b = -jnp.exp(A_log_b)[:, None] * jax.nn.softplus(a_b + dt_bias_b[:, None])
    rows = _row_spec(L1_HEADS_PER_STEP, ncs, nt)
    gdn_in = [(qkv, GDN_DK, 0), (qkv, GDN_DK, GDN_QK), (qkv, GDN_DV, 2 * GDN_QK)]
    kw = dict(delta=True, rotary=False, hb=L1_HEADS_PER_STEP)
    od_f, n_df = _scan_call("scalar", gdn_in, [(_row_layout(la_f), rows), (_row_layout(jax.nn.sigmoid(b_f)), rows)],
                            s_df, False, b, t, GDN_HEADS, GDN_DK, GDN_DV, **kw)
    od_b, n_db = _scan_call("scalar", gdn_in, [(_row_layout(la_b), rows), (_row_layout(jax.nn.sigmoid(b_b)), rows)],
                            s_db, True, b, t, GDN_HEADS, GDN_DK, GDN_DV, **kw)
    mu_perm = jnp.concatenate([mu[3 * RWKV_C:], jnp.zeros((PROJ_TN - RWKV_LORA_IN,), F32), mu[:3 * RWKV_C]])
    zr = _time_filter(z, L1_LORA, jnp.pad(mu_perm[None, :], ((0, 7), (0, 0))), b, t, "shift")
    o_r = PROJ_TN
    o_wd = 0
    o_ad = o_wd + 2 * RWKV_DECAY_LORA
    o_gd = o_ad + 2 * RWKV_AAA_LORA
    vec = lambda v: (v.reshape(1, RWKV_C),
                     lambda tile: pl.BlockSpec((1, L1_HEADS_PER_STEP * RWKV_N), lambda bi, hi, ti: (0, hi)))
    tct = min(TIME_TILE, t)
    lora_in = lambda cb: (zr, lambda tile: pl.BlockSpec((tct, 2 * RWKV_DECAY_LORA),
                                                        lambda bi, hi, ti: (bi * nt + tile(ti), cb)))
    lora_w = lambda w: (w, lambda tile: pl.BlockSpec((2 * RWKV_DECAY_LORA, L1_HEADS_PER_STEP * RWKV_N),
                                                     lambda bi, hi, ti: (0, hi)))
    toks = [(zr, RWKV_N, o_r), (zr, RWKV_N, o_r + RWKV_C), (zr, RWKV_N, o_r + 2 * RWKV_C)]
    outs = []
    for rev, (w0, w2, a0, a2), s0 in ((False, (w0_f, w2_f, a0_f, a2_f), s_wf), (True, (w0_b, w2_b, a0_b, a2_b), s_wb)):
        pad = ((RWKV_DECAY_LORA, 0), (0, 0)) if rev else ((0, RWKV_DECAY_LORA), (0, 0))
        extra = [lora_in(o_wd // (2 * RWKV_DECAY_LORA)), lora_in(o_ad // (2 * RWKV_AAA_LORA)),
                 lora_w(jnp.pad(w2, pad)), vec(w0), lora_w(jnp.pad(a2, pad)), vec(a0), vec(k_k), vec(k_a), vec(r_k)]
        outs.append(_scan_call("rwkv", toks, extra, s0, rev, b, t, RWKV_HEADS, RWKV_N, RWKV_N, n_tok_out=2,
                               hb=L1_HEADS_PER_STEP))
    (ow_f, bon_f, n_wf), (ow_b, bon_b, n_wb) = outs
    toks = [(od_f, GDN_V, 0), (od_b, GDN_V, 0), (ow_f, RWKV_C, 0), (ow_b, RWKV_C, 0), (bon_f, RWKV_C, 0),
            (bon_b, RWKV_C, 0), (z, GDN_V, L1_GG // GDN_V), (zr, RWKV_GATE_LORA, o_gd // RWKV_GATE_LORA)]
    vecs = [jnp.tile(gdn_norm, GDN_HEADS)[None, :], ln_w[None, :], ln_b[None, :], g2]
    x = _mix_out(_mix_out_l1_kernel, x, toks, gate_mod, tokens_per_mod, vecs, w_out)
    return x, (n_df, n_db, n_wf, n_wb)


def _permuted_w_in(w, pieces):
    cols = [jnp.zeros((w.shape[0], p), w.dtype) if isinstance(p, int) else w[:, p[0]:p[1]] for p in pieces]
    out = jnp.concatenate(cols, axis=1).astype(BF16)
    assert out.shape[1] % PROJ_TN == 0
    return out


def kernel(x_prompt, x_sample, c, c_ctx, state_l0_gla_fwd, state_l0_gla_bwd, state_l0_ret_fwd, state_l0_ret_bwd, state_l1_gdn_fwd, state_l1_gdn_bwd, state_l1_rwkv_fwd, state_l1_rwkv_bwd, l0_w_mod, l0_b_mod, l0_norm1, l0_norm2, l0_norm3, l0_ffn1_wg, l0_ffn1_wu, l0_ffn1_wd, l0_ffn2_wg, l0_ffn2_wu, l0_ffn2_wd, l0_w_in, l0_w_out, l0_gla_gk_up_fwd, l0_gla_gk_b_fwd, l0_gla_gk_up_bwd, l0_gla_gk_b_bwd, l0_gla_norm, l0_ret_norm, l1_w_mod, l1_b_mod, l1_norm1, l1_norm2, l1_norm3, l1_ffn1_wg, l1_ffn1_wu, l1_ffn1_wd, l1_ffn2_wg, l1_ffn2_wu, l1_ffn2_wd, l1_w_in, l1_w_out, l1_gdn_conv, l1_gdn_A_log_fwd, l1_gdn_dt_bias_fwd, l1_gdn_A_log_bwd, l1_gdn_dt_bias_bwd, l1_gdn_norm, l1_rwkv_mu, l1_rwkv_w0_fwd, l1_rwkv_w2_fwd, l1_rwkv_a0_fwd, l1_rwkv_a2_fwd, l1_rwkv_w0_bwd, l1_rwkv_w2_bwd, l1_rwkv_a0_bwd, l1_rwkv_a2_bwd, l1_rwkv_g2, l1_rwkv_k_k, l1_rwkv_k_a, l1_rwkv_r_k, l1_rwkv_ln_w, l1_rwkv_ln_b, final_norm):
    bp, tp, d = x_prompt.shape
    bs, ts, _ = x_sample.shape
    cond = jnp.concatenate([c_ctx[None, :], c, jnp.zeros((8 - 1 - bs, d), F32)], axis=0)
    common = (
        (l0_w_mod, l0_b_mod, (l0_norm1, l0_norm2, l0_norm3), (l0_ffn1_wg, l0_ffn1_wu, l0_ffn1_wd),
         (l0_ffn2_wg, l0_ffn2_wu, l0_ffn2_wd), l0_w_in, l0_w_out),
        (l1_w_mod, l1_b_mod, (l1_norm1, l1_norm2, l1_norm3), (l1_ffn1_wg, l1_ffn1_wu, l1_ffn1_wd),
         (l1_ffn2_wg, l1_ffn2_wu, l1_ffn2_wd), l1_w_in, l1_w_out),
    )
    caches = (
        (state_l0_gla_fwd, state_l0_gla_bwd, state_l0_ret_fwd, state_l0_ret_bwd),
        (state_l1_gdn_fwd, state_l1_gdn_bwd, state_l1_rwkv_fwd, state_l1_rwkv_bwd),
    )
    l0_pieces = [(0, 2 * GLA_QK + 2 * GLA_V), (L0_END - 2 * RET_QK - 2 * RET_V, L0_END),
                 (2 * GLA_QK + 2 * GLA_V, 2 * GLA_QK + 2 * GLA_V + 2 * GLA_LOWRANK)]
    l0_pieces.append((-L0_END) % PROJ_TN)
    l1_pieces = [(0, GDN_QKV + GDN_V), (GDN_IN + 3 * RWKV_C, GDN_IN + RWKV_IN), (GDN_QKV + GDN_V, GDN_IN),
                 PROJ_TN - RWKV_LORA_IN - 4 * GDN_HEADS, (GDN_IN, GDN_IN + 3 * RWKV_C)]
    w_in_perm = (_permuted_w_in(l0_w_in, l0_pieces), _permuted_w_in(l1_w_in, l1_pieces))
    l0_params = (l0_gla_gk_up_fwd, l0_gla_gk_b_fwd, l0_gla_gk_up_bwd, l0_gla_gk_b_bwd, l0_gla_norm, l0_ret_norm)
    l1_params = (l1_gdn_conv, l1_gdn_A_log_fwd, l1_gdn_dt_bias_fwd, l1_gdn_A_log_bwd, l1_gdn_dt_bias_bwd,
                 l1_gdn_norm, l1_rwkv_mu, l1_rwkv_w0_fwd, l1_rwkv_w2_fwd, l1_rwkv_a0_fwd, l1_rwkv_a2_fwd,
                 l1_rwkv_w0_bwd, l1_rwkv_w2_bwd, l1_rwkv_a0_bwd, l1_rwkv_a2_bwd,
                 l1_rwkv_g2, l1_rwkv_k_k, l1_rwkv_k_a, l1_rwkv_r_k, l1_rwkv_ln_w, l1_rwkv_ln_b)
    bf = lambda w: w.astype(BF16)
    xs = [x_prompt.reshape(bp * tp, d), x_sample.reshape(bs * ts, d)]
    geom = [(bp, tp, bp * tp, slice(0, 1)), (bs, ts, ts, slice(1, 1 + bs))]
    new_states = []
    for layer in range(2):
        w_mod, b_mod, norms, ffn1, ffn2, _, w_out = common[layer]
        m = _adaln(cond, w_mod, b_mod)
        mods = [t_.reshape(8, 1, d) for t_ in jnp.split(m, N_MOD, axis=-1)]
        ffn1 = tuple(bf(w) for w in ffn1)
        ffn2 = tuple(bf(w) for w in ffn2)
        w_out = bf(w_out)
        for gi in range(2):
            b, t, tpm, rows = geom[gi]
            sh1, sc1, g1, sh2, sc2, g2, sh3, sc3, g3 = [mm[rows] for mm in mods]
            x = _ffn(xs[gi], (sh1, sc1, g1), tpm, norms[0], *ffn1, final_norm, False)
            z = _proj_in(x, sh2, sc2, tpm, norms[1], w_in_perm[layer])
            cache = caches[layer] if gi == 1 else None
            if layer == 0:
                x, st = _mixer_gla_ret(x, z, b, t, cache, gi == 1, g2, tpm, w_out, *l0_params)
            else:
                x, st = _mixer_gdn_rwkv(x, z, b, t, cache, g2, tpm, w_out, *l1_params)
            if gi == 0:
                new_states.extend(st)
            xs[gi] = _ffn(x, (sh3, sc3, g3), tpm, norms[2], *ffn2, final_norm, layer == 1)
    return (xs[0].reshape(bp, tp, d), xs[1].reshape(bs, ts, d), *new_states)
```

```python
import functools

import numpy as np
import jax
import jax.numpy as jnp
from jax import lax
from jax.experimental import pallas as pl
from jax.experimental.pallas import tpu as pltpu

F32 = jnp.float32
BF16 = jnp.bfloat16

D_MODEL = 2048
D_FF = 5632
N_MOD = 9
NORM_EPS = 1e-6
GRID_W = 64
ROPE_BASE = 10000.0
CHUNK = 64
SUB = 16
TIME_TILE = 256
L1_HEADS_PER_STEP = 8

GLA_HEADS, GLA_DK, GLA_DV, GLA_LOWRANK, GLA_GATE_NORM = 4, 128, 256, 16, 16.0
GLA_QK, GLA_V = GLA_HEADS * GLA_DK, GLA_HEADS * GLA_DV
RET_HEADS, RET_DK, RET_DV = 4, 128, 256
RET_QK, RET_V = RET_HEADS * RET_DK, RET_HEADS * RET_DV
RET_DECAY_EXP_FWD, RET_DECAY_EXP_BWD = 5.0, 5.5
GDN_HEADS, GDN_DK, GDN_DV = 8, 128, 128
GDN_QK, GDN_V = GDN_HEADS * GDN_DK, GDN_HEADS * GDN_DV
GDN_QKV = 2 * GDN_QK + GDN_V
CONV_K = 5
GDN_IN = GDN_QKV + GDN_V + 4 * GDN_HEADS
RWKV_HEADS, RWKV_N = 16, 64
RWKV_C = RWKV_HEADS * RWKV_N
RWKV_DECAY_LORA, RWKV_AAA_LORA, RWKV_GATE_LORA = 64, 64, 128
RWKV_GN_EPS = 64e-5
RWKV_IN = 3 * RWKV_C + 2 * RWKV_DECAY_LORA + 2 * RWKV_AAA_LORA + RWKV_GATE_LORA

L0_GQ, L0_GK, L0_GV, L0_GG = 0, GLA_QK, 2 * GLA_QK, 2 * GLA_QK + GLA_V
L0_RQ = L0_GG + GLA_V
L0_RK, L0_RV = L0_RQ + RET_QK, L0_RQ + 2 * RET_QK
L0_RG = L0_RV + RET_V
L0_GDF = L0_RG + RET_V
L0_GDB = L0_GDF + GLA_LOWRANK
L0_END = L0_GDB + GLA_LOWRANK
L1_GG = GDN_QKV
L1_LORA = GDN_QKV + GDN_V
RWKV_LORA_IN = RWKV_IN - 3 * RWKV_C
L1_SCAL = L1_LORA + RWKV_LORA_IN
PROJ_TN = 512
L1_R = L1_LORA + PROJ_TN
L1_END = L1_R + 3 * RWKV_C
FILTER_BW = 512
FFN_TF = 512

VMEM_LIMIT_BYTES = 56 * 1024 * 1024


def _cparams(n_axes):
    return pltpu.CompilerParams(dimension_semantics=("arbitrary",) * n_axes, vmem_limit_bytes=VMEM_LIMIT_BYTES)


def _bdot(a, b):
    return jnp.dot(a.astype(BF16), b.astype(BF16), preferred_element_type=F32)


def _bmm(a, b):
    return lax.dot_general(a.astype(BF16), b.astype(BF16), (((2,), (1,)), ((0,), (0,))),
                           preferred_element_type=F32)


def _bmm_nt(a, b):
    return lax.dot_general(a.astype(BF16), b.astype(BF16), (((2,), (2,)), ((0,), (0,))),
                           preferred_element_type=F32)


def _bmm_tn(a, b):
    return lax.dot_general(a.astype(BF16), b.astype(BF16), (((1,), (1,)), ((0,), (0,))),
                           preferred_element_type=F32)


def _iota2(n, m, axis):
    return lax.broadcasted_iota(jnp.int32, (n, m), axis)


def _split3(x):
    x1 = x.astype(BF16)
    r1 = x - x1.astype(F32)
    x2 = r1.astype(BF16)
    x3 = (r1 - x2.astype(F32)).astype(BF16)
    return x1, x2, x3


def _tile_cumsum(x, rev):
    tc = x.shape[0]
    row = _iota2(tc, tc, 0)
    col = _iota2(tc, tc, 1)
    same = jnp.right_shift(row, 6) == jnp.right_shift(col, 6)
    tri = jnp.where(same & ((row >= col) if not rev else (row <= col)), 1.0, 0.0).astype(BF16)
    x1, x2, x3 = _split3(x)
    d = lambda y: jnp.dot(tri, y, preferred_element_type=F32)
    return d(x1) + (d(x2) + d(x3))


def _lane_group_sum(x, group):
    sh = group.bit_length() - 1
    ones_blk = jnp.where(jnp.right_shift(_iota2(128, 128, 0), sh) == jnp.right_shift(_iota2(128, 128, 1), sh),
                         1.0, 0.0).astype(BF16)
    x1, x2, x3 = _split3(x)
    d = lambda y: jnp.dot(y, ones_blk, preferred_element_type=F32)
    return jnp.concatenate([d(x1[:, j:j + 128]) + (d(x2[:, j:j + 128]) + d(x3[:, j:j + 128]))
                            for j in range(0, x.shape[1], 128)], axis=1)


def _chunk_masks(c, rev):
    row = _iota2(c, c, 0)
    col = _iota2(c, c, 1)
    eye = row == col
    incl = (row >= col) if not rev else (row <= col)
    strict = (row > col) if not rev else (row < col)
    same_blk = jnp.right_shift(row, 4) == jnp.right_shift(col, 4)
    return eye, incl, strict, same_blk


def _silu(x):
    return x * jax.nn.sigmoid(x)


def _softplus(x):
    return jnp.maximum(x, 0.0) + jnp.log1p(jnp.exp(-jnp.abs(x)))


def _split_heads(x, hb, ncs):
    d = x.shape[1] // hb
    return jnp.concatenate([x[:, h * d:(h + 1) * d].reshape(ncs, CHUNK, d) for h in range(hb)], axis=0)


def _gla_phase1(q, k, v, g, rev):
    nb, c, kd = q.shape
    row = _iota2(c, c, 0)
    col = _iota2(c, c, 1)
    ns = c // SUB
    rows = []
    for i in range(ns):
        first = (i == 0) if not rev else (i == ns - 1)
        if first:
            rows.append(jnp.zeros((nb, SUB, c), F32))
            continue
        r0 = i * SUB
        gref = g[:, r0 - 1:r0, :] if not rev else g[:, r0 + SUB:r0 + SUB + 1, :]
        qs = q[:, r0:r0 + SUB, :] * jnp.exp(g[:, r0:r0 + SUB, :] - gref)
        if not rev:
            ks = jnp.concatenate([k[:, :r0] * jnp.exp(gref - g[:, :r0]), jnp.zeros((nb, c - r0, kd), F32)], axis=1)
        else:
            ks = jnp.concatenate([jnp.zeros((nb, r0 + SUB, kd), F32),
                                  k[:, r0 + SUB:] * jnp.exp(gref - g[:, r0 + SUB:])], axis=1)
        rows.append(_bmm_nt(qs, ks))
    att = jnp.concatenate(rows, axis=1)
    blk_r = jnp.right_shift(row, 4)
    blk_c = jnp.right_shift(col, 4)
    dist = (row - col) if not rev else (col - row)
    dist = jnp.where(blk_r == blk_c, dist, -1)
    k2 = k.reshape(nb * c, kd)
    g2 = g.reshape(nb * c, kd)
    for d in range(SUB):
        if d == 0:
            ksh, gsh = k, g
        else:
            sh = d if not rev else nb * c - d
            ksh = pltpu.roll(k2, sh, 0).reshape(nb, c, kd)
            gsh = pltpu.roll(g2, sh, 0).reshape(nb, c, kd)
        e = jnp.exp(jnp.minimum(g - gsh, 0.0))
        band = jnp.sum(q * ksh * e, axis=2, keepdims=True)
        att = jnp.where(dist == d, band, att)
    g_end = g[:, c - 1:c, :] if not rev else g[:, 0:1, :]
    eye_k = _iota2(kd, kd, 0) == _iota2(kd, kd, 1)
    dcol = jnp.sum(jnp.where(eye_k, jnp.exp(g_end), 0.0), axis=2, keepdims=True)
    return q * jnp.exp(g), _bmm(att, v), _bmm_tn(k * jnp.exp(g_end - g), v), dcol


def _gla_kernel(*refs, rev, has_s0, hb, ncs):
    q_ref, k_ref, v_ref, gd_ref, w_ref, b_ref = refs[:6]
    pos = 6
    s0_ref = None
    if has_s0:
        s0_ref = refs[pos]
        pos += 1
    o_ref, sout_ref, s_scr = refs[pos], refs[pos + 1], refs[pos + 2]
    _init_state(s_scr, s0_ref)
    sp = lambda x: _split_heads(x, hb, ncs)
    la = -_softplus(-(_bdot(gd_ref[...], w_ref[...]) + b_ref[...])) * (1.0 / GLA_GATE_NORM)
    r2, y0, s_add, dec = _gla_phase1(sp(q_ref[...] * GLA_DK ** -0.5), sp(k_ref[...]), sp(v_ref[...]),
                                     sp(_tile_cumsum(la, rev)), rev)
    s = _phase2(r2, y0, None, 1.0, s_add, dec, s_scr[...], o_ref, hb, ncs, rev)
    s_scr[...] = s

    @pl.when(pl.program_id(2) == pl.num_programs(2) - 1)
    def _():
        sout_ref[0] = s


def _tri_inverse_b(n_mat, eye, same_blk):
    eyef = jnp.where(eye, 1.0, 0.0)
    nd = jnp.where(same_blk, n_mat, 0.0)
    no = n_mat - nd
    p = eyef - nd
    m = nd
    for _ in range(3):
        m = _bmm(m, m)
        p = p + _bmm(p, m)
    mm = _bmm(p, no)
    q = eyef - mm
    q = q + _bmm(q, _bmm(mm, mm))
    return _bmm(q, p)


def _rwkv_phase1(r, lw, g, k, v, a, b, rev):
    nb, c, n = r.shape
    eye, incl, strict, same_blk = _chunk_masks(c, rev)
    eg = jnp.exp(g)
    einv = jnp.exp(-g)
    at = a * jnp.exp(g - lw)
    rt = r * eg
    ar = jnp.concatenate([at, rt], axis=1)
    ab = _bmm_nt(ar, b * einv)
    ak = _bmm_nt(ar, k * einv)
    a_ab = jnp.where(strict, ab[:, :c], 0.0)
    a_rb = jnp.where(incl, ab[:, c:], 0.0)
    a_ak = jnp.where(strict, ak[:, :c], 0.0)
    a_rk = jnp.where(incl, ak[:, c:], 0.0)
    t_inv = _tri_inverse_b(-a_ab, eye, same_blk)
    w = _bmm(t_inv, at)
    u0 = _bmm(t_inv, _bmm(a_ak, v))
    r2 = rt + _bmm(a_rb, w)
    y0 = _bmm(a_rk, v) + _bmm(a_rb, u0)
    g_end = g[:, c - 1:c, :] if not rev else g[:, 0:1, :]
    dec = jnp.exp(g_end - g)
    bd = b * dec
    m = _bmm_tn(bd, w)
    s_add = _bmm_tn(bd, u0) + _bmm_tn(k * dec, v)
    eye_n = _iota2(n, n, 0) == _iota2(n, n, 1)
    dcol = jnp.sum(jnp.where(eye_n, jnp.exp(g_end), 0.0), axis=2, keepdims=True)
    return r2, y0, m, s_add, dcol


def _scalar_phase1(q, k, v, lc_row, beta_row, rev, delta):
    nb, c, _ = q.shape
    eye, incl, strict, same_blk = _chunk_masks(c, rev)
    row = _iota2(c, c, 0)
    col = _iota2(c, c, 1)
    before = (row <= col) if not rev else (row >= col)
    to_col = lambda x: jnp.sum(jnp.where(eye, x, 0.0), axis=2, keepdims=True)
    lc_col = to_col(lc_row)
    g_row = jnp.sum(jnp.where(before, lc_col, 0.0), axis=1, keepdims=True)
    g_col = to_col(g_row)
    rel = jnp.where(incl, jnp.exp(jnp.minimum(g_col - g_row, 0.0)), 0.0)
    eg = jnp.exp(g_col)
    g_end = g_col[:, c - 1:c, :] if not rev else g_col[:, 0:1, :]
    kdec = k * jnp.exp(g_end - g_col)
    if delta:
        beta_col = to_col(beta_row)
        qkk = _bmm_nt(jnp.concatenate([q, k], axis=1), k)
        p = qkk[:, :c] * rel
        n_mat = jnp.where(strict, beta_col * rel * qkk[:, c:], 0.0)
        t_inv = _tri_inverse_b(n_mat, eye, same_blk)
        sol_v = _bmm(t_inv, beta_col * v)
        sol_k = _bmm(t_inv, (beta_col * eg) * k)
        r2 = q * eg - _bmm(p, sol_k)
        y0 = _bmm(p, sol_v)
        m = _bmm_tn(kdec, sol_k)
        s_add = _bmm_tn(kdec, sol_v)
    else:
        p = _bmm_nt(q, k) * rel
        r2 = q * eg
        y0 = _bmm(p, v)
        m = None
        s_add = _bmm_tn(kdec, v)
    return r2, y0, m, s_add, jnp.exp(g_end)


def _phase2(r2, y0, m, m_sign, s_add, dec, s, o_ref, hb, ncs, rev):
    unb = lambda x: x.reshape((hb, ncs) + x.shape[1:])
    r2, y0, s_add, dec = unb(r2), unb(y0), unb(s_add), unb(dec)
    if m is not None:
        m = unb(m)
    for ci in range(ncs):
        cc = (ncs - 1 - ci) if rev else ci
        y = y0[:, cc] + _bmm(r2[:, cc], s)
        o_ref[cc * CHUNK:(cc + 1) * CHUNK, :] = jnp.concatenate([y[h] for h in range(hb)], axis=1)
        s_new = dec[:, cc] * s + s_add[:, cc]
        if m is not None:
            s_new = s_new + m_sign * _bmm(m[:, cc], s)
        s = s_new
    return s


def _init_state(s_scr, s0_ref):
    @pl.when(pl.program_id(2) == 0)
    def _():
        if s0_ref is not None:
            s_scr[...] = s0_ref[0]
        else:
            s_scr[...] = jnp.zeros(s_scr.shape, F32)


def _scalar_kernel(*refs, rev, has_s0, hb, ncs, delta, rotary):
    q_ref, k_ref, v_ref, la_ref, be_ref = refs[:5]
    pos = 5
    cos_ref = sin_ref = s0_ref = None
    if rotary:
        cos_ref, sin_ref = refs[pos], refs[pos + 1]
        pos += 2
    if has_s0:
        s0_ref = refs[pos]
        pos += 1
    o_ref, sout_ref, s_scr = refs[pos], refs[pos + 1], refs[pos + 2]
    _init_state(s_scr, s0_ref)
    nb = hb * ncs
    q = q_ref[...]
    k = k_ref[...]
    dk = q.shape[1] // hb
    if rotary:
        cos2, sin2 = cos_ref[...], sin_ref[...]
        rot = lambda x: jnp.concatenate(
            [x[:, h * dk:(h + 1) * dk] * cos2 + pltpu.roll(x[:, h * dk:(h + 1) * dk], dk // 2, 1) * sin2
             for h in range(hb)], axis=1)
        q, k = rot(q), rot(k)
    q = _split_heads(q, hb, ncs)
    k = _split_heads(k, hb, ncs)
    if delta:
        l2norm = lambda x: x * lax.rsqrt(jnp.sum(x * x, axis=-1, keepdims=True) + NORM_EPS)
        q, k = l2norm(q), l2norm(k)
    q = q * dk ** -0.5
    v = _split_heads(v_ref[...], hb, ncs)
    r2, y0, m, s_add, dec = _scalar_phase1(q, k, v, la_ref[0].reshape(nb, 1, CHUNK),
                                           be_ref[0].reshape(nb, 1, CHUNK), rev, delta)
    s = _phase2(r2, y0, m, -1.0, s_add, dec, s_scr[...], o_ref, hb, ncs, rev)
    s_scr[...] = s

    @pl.when(pl.program_id(2) == pl.num_programs(2) - 1)
    def _():
        sout_ref[0] = s


def _rwkv_kernel(*refs, rev, has_s0, hb, ncs):
    r_ref, kr_ref, v_ref, wd_ref, ad_ref, w2_ref, w0_ref, a2_ref, a0_ref, kk_ref, ka_ref, rk_ref = refs[:12]
    pos = 12
    s0_ref = None
    if has_s0:
        s0_ref = refs[pos]
        pos += 1
    o_ref, bonus_ref, sout_ref, s_scr = refs[pos], refs[pos + 1], refs[pos + 2], refs[pos + 3]
    _init_state(s_scr, s0_ref)
    r = r_ref[...]
    kr = kr_ref[...]
    lw = -jnp.exp(-_softplus(-(_bdot(jnp.tanh(wd_ref[...]), w2_ref[...]) + w0_ref[...])) - 0.5)
    a = jax.nn.sigmoid(_bdot(ad_ref[...], a2_ref[...]) + a0_ref[...])
    kd = kr * (1.0 + (a - 1.0) * ka_ref[...])
    sp = lambda x: _split_heads(x, hb, ncs)
    kk = kr * kk_ref[...]
    kk = kk * lax.rsqrt(_lane_group_sum(kk * kk, RWKV_N) + NORM_EPS)
    v = v_ref[...]
    bonus_ref[...] = _lane_group_sum(r * kd * rk_ref[...], RWKV_N) * v
    r2, y0, m, s_add, dec = _rwkv_phase1(sp(r), sp(lw), sp(_tile_cumsum(lw, rev)), sp(kd), sp(v), sp(-kk),
                                         sp(kk * a), rev)
    s = _phase2(r2, y0, m, 1.0, s_add, dec, s_scr[...], o_ref, hb, ncs, rev)
    s_scr[...] = s

    @pl.when(pl.program_id(2) == pl.num_programs(2) - 1)
    def _():
        sout_ref[0] = s


def _scan_call(kind, tok_ins, extra_ins, s0, rev, b, t, heads, dk, dv, n_tok_out=1, hb=4, **kw):
    tc = min(TIME_TILE, t)
    nt = t // tc
    ncs = tc // CHUNK
    tile = lambda ti: (nt - 1 - ti) if rev else ti

    def tok_spec(width, off):
        bw = hb * width
        assert off % bw == 0
        cb = off // bw
        return pl.BlockSpec((tc, bw), lambda bi, hi, ti: (bi * nt + tile(ti), cb + hi))

    in_specs = [tok_spec(w, off) for _, w, off in tok_ins] + [mk(tile) for _, mk in extra_ins]
    args = [x for x, _, _ in tok_ins] + [x for x, _ in extra_ins]
    state_spec = pl.BlockSpec((1, hb, dk, dv), lambda bi, hi, ti: (bi, hi, 0, 0))
    if s0 is not None:
        in_specs.append(state_spec)
        args.append(s0)
    body = {"gla": _gla_kernel, "scalar": _scalar_kernel, "rwkv": _rwkv_kernel}[kind]
    outs = pl.pallas_call(
        functools.partial(body, rev=rev, has_s0=s0 is not None, hb=hb, ncs=ncs, **kw),
        grid=(b, heads // hb, nt),
        in_specs=in_specs,
        out_specs=[tok_spec(dv, 0)] * n_tok_out + [state_spec],
        out_shape=[jax.ShapeDtypeStruct((b * t, heads * dv), F32)] * n_tok_out
        + [jax.ShapeDtypeStruct((b, heads, dk, dv), F32)],
        scratch_shapes=[pltpu.VMEM((hb, dk, dv), F32)],
        compiler_params=_cparams(3),
        name=f"scan_{kind}_{'bwd' if rev else 'fwd'}",
    )(*args)
    return outs


def _row_spec(hb, ncs, nt):
    return lambda tile: pl.BlockSpec((1, hb, ncs, 1, CHUNK), lambda bi, hi, ti: (bi, hi, tile(ti), 0, 0))


def _row_layout(x):
    b, h, t = x.shape
    return x.reshape(b, h, t // CHUNK, 1, CHUNK)


def _modulated_norm(x, nw, shift, scale):
    y = x * lax.rsqrt(jnp.mean(x * x, axis=-1, keepdims=True) + NORM_EPS) * nw
    return y * (1.0 + scale) + shift


def _adaln_kernel(c_ref, w_ref, b_ref, o_ref):
    o_ref[...] = _bdot(_silu(c_ref[...]), w_ref[...]) + b_ref[...]


def _adaln(cond, w_mod, b_mod):
    r, d = cond.shape
    n = w_mod.shape[1]
    tn = 1024
    return pl.pallas_call(
        _adaln_kernel,
        grid=(n // tn,),
        in_specs=[pl.BlockSpec((r, d), lambda j: (0, 0)),
                  pl.BlockSpec((d, tn), lambda j: (0, j)),
                  pl.BlockSpec((1, tn), lambda j: (0, j))],
        out_specs=pl.BlockSpec((r, tn), lambda j: (0, j)),
        out_shape=jax.ShapeDtypeStruct((r, n), F32),
        compiler_params=_cparams(1),
        name="adaln",
    )(cond, w_mod, b_mod.reshape(1, n))


def _ffn_kernel(x_ref, sh_ref, sc_ref, g_ref, nw_ref, wgu_ref, wd_ref, fn_ref, o_ref, h_scr, *,
                nf, final_norm):
    f = pl.program_id(1)

    @pl.when(f == 0)
    def _():
        h = _modulated_norm(x_ref[...], nw_ref[...], sh_ref[0], sc_ref[0])
        h_scr[...] = h.astype(BF16)
        o_ref[...] = jnp.zeros(o_ref.shape, F32)

    gu = jnp.dot(h_scr[...], wgu_ref[...], preferred_element_type=F32)
    tf = gu.shape[1] // 2
    o_ref[...] += jnp.dot((_silu(gu[:, :tf]) * gu[:, tf:]).astype(BF16), wd_ref[...], preferred_element_type=F32)

    @pl.when(f == nf - 1)
    def _():
        y = x_ref[...] + (0.5 * g_ref[0]) * o_ref[...]
        if final_norm:
            y = y * lax.rsqrt(jnp.mean(y * y, axis=-1, keepdims=True) + NORM_EPS) * fn_ref[...]
        o_ref[...] = y


def _ffn(x, mods, tokens_per_mod, nw, wgu, wd, fnorm, final_norm, tm=512, tf=FFN_TF):
    n, d = x.shape
    shift, scale, gate = mods
    tm = min(tm, tokens_per_mod)
    nf = wgu.shape[1] // (2 * tf)
    mod_spec = pl.BlockSpec((1, 1, d), lambda i, f: ((i * tm) // tokens_per_mod, 0, 0))
    vec_spec = pl.BlockSpec((1, d), lambda i, f: (0, 0))
    return pl.pallas_call(
        functools.partial(_ffn_kernel, nf=nf, final_norm=final_norm),
        grid=(n // tm, nf),
        in_specs=[pl.BlockSpec((tm, d), lambda i, f: (i, 0)), mod_spec, mod_spec, mod_spec, vec_spec,
                  pl.BlockSpec((d, 2 * tf), lambda i, f: (0, f)),
                  pl.BlockSpec((tf, d), lambda i, f: (f, 0)),
                  vec_spec],
        out_specs=pl.BlockSpec((tm, d), lambda i, f: (i, 0)),
        out_shape=jax.ShapeDtypeStruct((n, d), F32),
        scratch_shapes=[pltpu.VMEM((tm, d), BF16)],
        compiler_params=_cparams(2),
        name="ffn",
    )(x, shift, scale, gate, nw.reshape(1, d), wgu, wd, fnorm.reshape(1, d))


def _ffn_weights(wg, wu, wd):
    d, dff = wg.shape
    nf = dff // FFN_TF
    wgu = jnp.stack([wg.reshape(d, nf, FFN_TF), wu.reshape(d, nf, FFN_TF)], axis=2).reshape(d, 2 * dff)
    return wgu.astype(BF16), wd.astype(BF16)


def _proj_in_kernel(x_ref, sh_ref, sc_ref, nw_ref, w_ref, o_ref, h_scr):
    @pl.when(pl.program_id(1) == 0)
    def _():
        h_scr[...] = _modulated_norm(x_ref[...], nw_ref[...], sh_ref[0], sc_ref[0]).astype(BF16)

    o_ref[...] = jnp.dot(h_scr[...], w_ref[...], preferred_element_type=F32)


def _proj_in(x, shift, scale, tokens_per_mod, nw, w, tm=1024, tn=PROJ_TN):
    n, d = x.shape
    n_out = w.shape[1]
    tm = min(tm, tokens_per_mod)
    mod_spec = pl.BlockSpec((1, 1, d), lambda i, j: ((i * tm) // tokens_per_mod, 0, 0))
    return pl.pallas_call(
        _proj_in_kernel,
        grid=(n // tm, n_out // tn),
        in_specs=[pl.BlockSpec((tm, d), lambda i, j: (i, 0)), mod_spec, mod_spec,
                  pl.BlockSpec((1, d), lambda i, j: (0, 0)),
                  pl.BlockSpec((d, tn), lambda i, j: (0, j))],
        out_specs=pl.BlockSpec((tm, tn), lambda i, j: (i, j)),
        out_shape=jax.ShapeDtypeStruct((n, n_out), F32),
        scratch_shapes=[pltpu.VMEM((tm, d), BF16)],
        compiler_params=_cparams(2),
        name="proj_in",
    )(x, shift, scale, nw.reshape(1, d), w)


def _head_norm_lanes(x, nheads, eps, center):
    d = x.shape[1] // nheads
    outs = []
    for h in range(nheads):
        xh = x[:, h * d:(h + 1) * d]
        if center:
            xh = xh - jnp.mean(xh, axis=-1, keepdims=True)
        outs.append(xh * lax.rsqrt(jnp.mean(xh * xh, axis=-1, keepdims=True) + eps))
    return jnp.concatenate(outs, axis=1)


def _head_norm_half_lanes(x, eps):
    lo = _iota2(1, 128, 1) < 64
    outs = []
    for j in range(x.shape[1] // 128):
        xb = x[:, j * 128:(j + 1) * 128]
        half = lambda y: jnp.where(lo, jnp.sum(jnp.where(lo, y, 0.0), axis=-1, keepdims=True),
                                   jnp.sum(jnp.where(lo, 0.0, y), axis=-1, keepdims=True)) * (1.0 / 64.0)
        xc = xb - half(xb)
        outs.append(xc * lax.rsqrt(half(xc * xc) + eps))
    return jnp.concatenate(outs, axis=1)


def _mix_out_l0_kernel(x_ref, gf_ref, gb_ref, rf_ref, rb_ref, gg_ref, rg_ref, g_ref, gn_ref, rn_ref, w_ref, o_ref):
    og = _head_norm_lanes(gf_ref[...] + gb_ref[...], GLA_HEADS, NORM_EPS, False) * gn_ref[...] * _silu(gg_ref[...])
    orr = _head_norm_lanes(rf_ref[...] + rb_ref[...], RET_HEADS, NORM_EPS, True) * rn_ref[...] * _silu(rg_ref[...])
    u = jnp.concatenate([og, orr], axis=1).astype(BF16)
    o_ref[...] = x_ref[...] + g_ref[0] * jnp.dot(u, w_ref[...], preferred_element_type=F32)


def _mix_out_l1_kernel(x_ref, df_ref, db_ref, wf_ref, wb_ref, bf_ref, bb_ref, gg_ref, gd_ref, g_ref,
                       dn_ref, lnw_ref, lnb_ref, g2_ref, w_ref, o_ref):
    od = _head_norm_lanes(df_ref[...] + db_ref[...], GDN_HEADS, NORM_EPS, False) * dn_ref[...] * _silu(gg_ref[...])
    y = _head_norm_half_lanes(wf_ref[...] + wb_ref[...], RWKV_GN_EPS) * lnw_ref[...] + lnb_ref[...]
    y = (y + bf_ref[...] + bb_ref[...]) * _bdot(jax.nn.sigmoid(gd_ref[...]), g2_ref[...])
    u = jnp.concatenate([od, y], axis=1).astype(BF16)
    o_ref[...] = x_ref[...] + g_ref[0] * jnp.dot(u, w_ref[...], preferred_element_type=F32)


def _mix_out(kern, x, tok_ins, gate_mod, tokens_per_mod, vecs, w, tm=256):
    n, d = x.shape
    in_specs = [pl.BlockSpec((tm, d), lambda i: (i, 0))]
    in_specs += [pl.BlockSpec((tm, bw), functools.partial(lambda i, cb: (i, cb), cb=cb)) for _, bw, cb in tok_ins]
    in_specs += [pl.BlockSpec((1, 1, d), lambda i: ((i * tm) // tokens_per_mod, 0, 0))]
    in_specs += [pl.BlockSpec(v.shape, lambda i: (0, 0)) for v in vecs]
    in_specs += [pl.BlockSpec(w.shape, lambda i: (0, 0))]
    return pl.pallas_call(
        kern,
        grid=(n // tm,),
        in_specs=in_specs,
        out_specs=pl.BlockSpec((tm, d), lambda i: (i, 0)),
        out_shape=jax.ShapeDtypeStruct((n, d), F32),
        compiler_params=_cparams(1),
        name="mix_out",
    )(x, *[a for a, _, _ in tok_ins], gate_mod, *vecs, w)


def _rotary_tables(t, dk):
    rows = t // GRID_W
    row = jnp.broadcast_to(jnp.arange(rows, dtype=F32)[:, None], (rows, GRID_W)).reshape(t)
    col = jnp.broadcast_to(jnp.arange(GRID_W, dtype=F32)[None, :], (rows, GRID_W)).reshape(t)
    quarter = dk // 4
    inv = ROPE_BASE ** (-jnp.arange(quarter, dtype=F32) / quarter)
    ang = jnp.concatenate([row[:, None] * inv, col[:, None] * inv], axis=-1)
    cos, sin = jnp.cos(ang), jnp.sin(ang)
    return jnp.concatenate([cos, cos], axis=-1), jnp.concatenate([-sin, sin], axis=-1)


def _retention_log_decay(exp0):
    h = jnp.arange(RET_HEADS, dtype=F32)
    return jnp.log1p(-jnp.power(2.0, -(exp0 + h)))


def _mixer_gla_ret(x, z, b, t, states, latent, gate_mod, tokens_per_mod, w_out, gk_up_f, gk_b_f, gk_up_b, gk_b_b,
                   gla_norm, ret_norm):
    nt = t // min(TIME_TILE, t)
    ncs = min(TIME_TILE, t) // CHUNK
    s_gf, s_gb, s_rf, s_rb = states if states is not None else (None,) * 4
    tct = min(TIME_TILE, t)
    gla_in = [(z, GLA_DK, L0_GQ), (z, GLA_DK, L0_GK), (z, GLA_DV, L0_GV)]
    full = lambda x: (x, lambda tile: pl.BlockSpec(x.shape, lambda bi, hi, ti: (0, 0)))
    gd_in = (z, lambda tile: pl.BlockSpec((tct, 128), lambda bi, hi, ti: (bi * nt + tile(ti), L0_GDF // 128)))
    gate_w = lambda w, lo: full(jnp.pad(w, ((lo, 128 - lo - GLA_LOWRANK), (0, 0))))
    og_f, n_gf = _scan_call("gla", gla_in, [gd_in, gate_w(gk_up_f, 0), full(gk_b_f[None, :])], s_gf, False, b, t,
                            GLA_HEADS, GLA_DK, GLA_DV)
    og_b, n_gb = _scan_call("gla", gla_in, [gd_in, gate_w(gk_up_b, GLA_LOWRANK), full(gk_b_b[None, :])], s_gb, True,
                            b, t, GLA_HEADS, GLA_DK, GLA_DV)
    ones = jnp.ones((b, RET_HEADS, t), F32)
    rows = _row_spec(RET_HEADS, ncs, nt)
    extra = lambda exp0: [(_row_layout(ones * _retention_log_decay(exp0)[None, :, None]), rows),
                          (_row_layout(ones), rows)]
    rot = []
    if latent:
        cos2, sin2 = _rotary_tables(t, RET_DK)
        tab = lambda tile: pl.BlockSpec((min(TIME_TILE, t), RET_DK), lambda bi, hi, ti: (tile(ti), 0))
        rot = [(cos2, tab), (sin2, tab)]
    ret_in = [(z, RET_DK, L0_RQ), (z, RET_DK, L0_RK), (z, RET_DV, L0_RV)]
    kw = dict(delta=False, rotary=latent)
    or_f, n_rf = _scan_call("scalar", ret_in, extra(RET_DECAY_EXP_FWD) + rot, s_rf, False, b, t, RET_HEADS,
                            RET_DK, RET_DV, **kw)
    or_b, n_rb = _scan_call("scalar", ret_in, extra(RET_DECAY_EXP_BWD) + rot, s_rb, True, b, t, RET_HEADS,
                            RET_DK, RET_DV, **kw)
    toks = [(og_f, GLA_V, 0), (og_b, GLA_V, 0), (or_f, RET_V, 0), (or_b, RET_V, 0),
            (z, GLA_V, L0_GG // GLA_V), (z, RET_V, L0_RG // RET_V)]
    vecs = [jnp.tile(gla_norm, GLA_HEADS)[None, :], jnp.tile(ret_norm, RET_HEADS)[None, :]]
    x = _mix_out(_mix_out_l0_kernel, x, toks, gate_mod, tokens_per_mod, vecs, w_out)
    return x, (n_gf, n_gb, n_rf, n_rb)


HALO = 8


def _time_filter_kernel(x_ref, prev_ref, next_ref, p_ref, o_ref, *, mode):
    ti = pl.program_id(1)
    x = x_ref[...]
    tc = x.shape[0]
    prv = jnp.where(ti > 0, prev_ref[...], 0.0)
    nxt = jnp.where(ti < pl.num_programs(1) - 1, next_ref[...], 0.0)
    e = jnp.concatenate([x, nxt, prv], axis=0)
    ne = tc + 2 * HALO
    at = lambda s: pltpu.roll(e, (ne - s) % ne, 0)[:tc]
    p = p_ref[...]
    if mode == "conv":
        acc = x * p[CONV_K // 2:CONV_K // 2 + 1]
        for j in range(CONV_K):
            if j != CONV_K // 2:
                acc = acc + at(j - CONV_K // 2) * p[j:j + 1]
        o_ref[...] = _silu(acc)
    else:
        o_ref[...] = x + p[0:1] * (0.5 * (at(-1) + at(1)) - x)


def _time_filter(z, col0, params, b, t, mode, tc=1024):
    n = z.shape[0]
    width = params.shape[1]
    bw = FILTER_BW
    assert col0 % bw == 0 and width % bw == 0
    tc = min(tc, t)
    nt = t // tc
    cb0 = col0 // bw
    hb_rows = tc // HALO
    return pl.pallas_call(
        functools.partial(_time_filter_kernel, mode=mode),
        grid=(b, nt, width // bw),
        in_specs=[pl.BlockSpec((tc, bw), lambda bi, ti, ci: (bi * nt + ti, cb0 + ci)),
                  pl.BlockSpec((HALO, bw), lambda bi, ti, ci: (jnp.maximum((bi * nt + ti) * hb_rows - 1, 0),
                                                               cb0 + ci)),
                  pl.BlockSpec((HALO, bw), lambda bi, ti, ci: (jnp.minimum((bi * nt + ti + 1) * hb_rows,
                                                                           n // HALO - 1), cb0 + ci)),
                  pl.BlockSpec((8, bw), lambda bi, ti, ci: (0, ci))],
        out_specs=pl.BlockSpec((tc, bw), lambda bi, ti, ci: (bi * nt + ti, ci)),
        out_shape=jax.ShapeDtypeStruct((n, width), F32),
        compiler_params=_cparams(3),
        name=f"time_filter_{mode}",
    )(z, z, z, params)


def _mixer_gdn_rwkv(x, z, b, t, states, gate_mod, tokens_per_mod, w_out, conv_w, A_log_f, dt_bias_f, A_log_b,
                    dt_bias_b, gdn_norm, mu, w0_f, w2_f, a0_f, a2_f, w0_b, w2_b, a0_b, a2_b, g2, k_k, k_a, r_k,
                    ln_w, ln_b):
    n = b * t
    nt = t // min(TIME_TILE, t)
    ncs = min(TIME_TILE, t) // CHUNK
    s_df, s_db, s_wf, s_wb = states if states is not None else (None,) * 4
    qkv = _time_filter(z, 0, jnp.pad(conv_w, ((0, 8 - CONV_K), (0, 0))), b, t, "conv")
    scal = z[:, L1_SCAL:L1_SCAL + 4 * GDN_HEADS].reshape(b, t, 4 * GDN_HEADS)
    a_f, a_b, b_f, b_b = jnp.split(jnp.swapaxes(scal, 1, 2), 4, axis=1)
    la_f = -jnp.exp(A_log_f)[:, None] * jax.nn.softplus(a_f + dt_bias_f[:, None])
    la_b = -jnp.exp(A_log_b)[:, None] * jax.nn.softplus(a_b + dt_bias_b[:, None])
    rows = _row_spec(L1_HEADS_PER_STEP, ncs, nt)
    gdn_in = [(qkv, GDN_DK, 0), (qkv, GDN_DK, GDN_QK), (qkv, GDN_DV, 2 * GDN_QK)]
    kw = dict(delta=True, rotary=False, hb=L1_HEADS_PER_STEP)
    od_f, n_df = _scan_call("scalar", gdn_in, [(_row_layout(la_f), rows), (_row_layout(jax.nn.sigmoid(b_f)), rows)],
                            s_df, False, b, t, GDN_HEADS, GDN_DK, GDN_DV, **kw)
    od_b, n_db = _scan_call("scalar", gdn_in, [(_row_layout(la_b), rows), (_row_layout(jax.nn.sigmoid(b_b)), rows)],
                            s_db, True, b, t, GDN_HEADS, GDN_DK, GDN_DV, **kw)
    mu_perm = jnp.concatenate([mu[3 * RWKV_C:], jnp.zeros((PROJ_TN - RWKV_LORA_IN,), F32), mu[:3 * RWKV_C]])
    zr = _time_filter(z, L1_LORA, jnp.pad(mu_perm[None, :], ((0, 7), (0, 0))), b, t, "shift")
    o_r = PROJ_TN
    o_wd = 0
    o_ad = o_wd + 2 * RWKV_DECAY_LORA
    o_gd = o_ad + 2 * RWKV_AAA_LORA
    vec = lambda v: (v.reshape(1, RWKV_C),
                     lambda tile: pl.BlockSpec((1, L1_HEADS_PER_STEP * RWKV_N), lambda bi, hi, ti: (0, hi)))
    tct = min(TIME_TILE, t)
    lora_in = lambda cb: (zr, lambda tile: pl.BlockSpec((tct, 2 * RWKV_DECAY_LORA),
                                                        lambda bi, hi, ti: (bi * nt + tile(ti), cb)))
    lora_w = lambda w: (w, lambda tile: pl.BlockSpec((2 * RWKV_DECAY_LORA, L1_HEADS_PER_STEP * RWKV_N),
                                                     lambda bi, hi, ti: (0, hi)))
    toks = [(zr, RWKV_N, o_r), (zr, RWKV_N, o_r + RWKV_C), (zr, RWKV_N, o_r + 2 * RWKV_C)]
    outs = []
    for rev, (w0, w2, a0, a2), s0 in ((False, (w0_f, w2_f, a0_f, a2_f), s_wf), (True, (w0_b, w2_b, a0_b, a2_b), s_wb)):
        pad = ((RWKV_DECAY_LORA, 0), (0, 0)) if rev else ((0, RWKV_DECAY_LORA), (0, 0))
        extra = [lora_in(o_wd // (2 * RWKV_DECAY_LORA)), lora_in(o_ad // (2 * RWKV_AAA_LORA)),
                 lora_w(jnp.pad(w2, pad)), vec(w0), lora_w(jnp.pad(a2, pad)), vec(a0), vec(k_k), vec(k_a), vec(r_k)]
        outs.append(_scan_call("rwkv", toks, extra, s0, rev, b, t, RWKV_HEADS, RWKV_N, RWKV_N, n_tok_out=2,
                               hb=L1_HEADS_PER_STEP))
    (ow_f, bon_f, n_wf), (ow_b, bon_b, n_wb) = outs
    toks = [(od_f, GDN_V, 0), (od_b, GDN_V, 0), (ow_f, RWKV_C, 0), (ow_b, RWKV_C, 0), (bon_f, RWKV_C, 0),
            (bon_b, RWKV_C, 0), (z, GDN_V, L1_GG // GDN_V), (zr, RWKV_GATE_LORA, o_gd // RWKV_GATE_LORA)]
    vecs = [jnp.tile(gdn_norm, GDN_HEADS)[None, :], ln_w[None, :], ln_b[None, :], g2]
    x = _mix_out(_mix_out_l1_kernel, x, toks, gate_mod, tokens_per_mod, vecs, w_out)
    return x, (n_df, n_db, n_wf, n_wb)


def _permuted_w_in(w, pieces):
    cols = [jnp.zeros((w.shape[0], p), w.dtype) if isinstance(p, int) else w[:, p[0]:p[1]] for p in pieces]
    out = jnp.concatenate(cols, axis=1).astype(BF16)
    assert out.shape[1] % PROJ_TN == 0
    return out


def kernel(x_prompt, x_sample, c, c_ctx, state_l0_gla_fwd, state_l0_gla_bwd, state_l0_ret_fwd, state_l0_ret_bwd, state_l1_gdn_fwd, state_l1_gdn_bwd, state_l1_rwkv_fwd, state_l1_rwkv_bwd, l0_w_mod, l0_b_mod, l0_norm1, l0_norm2, l0_norm3, l0_ffn1_wg, l0_ffn1_wu, l0_ffn1_wd, l0_ffn2_wg, l0_ffn2_wu, l0_ffn2_wd, l0_w_in, l0_w_out, l0_gla_gk_up_fwd, l0_gla_gk_b_fwd, l0_gla_gk_up_bwd, l0_gla_gk_b_bwd, l0_gla_norm, l0_ret_norm, l1_w_mod, l1_b_mod, l1_norm1, l1_norm2, l1_norm3, l1_ffn1_wg, l1_ffn1_wu, l1_ffn1_wd, l1_ffn2_wg, l1_ffn2_wu, l1_ffn2_wd, l1_w_in, l1_w_out, l1_gdn_conv, l1_gdn_A_log_fwd, l1_gdn_dt_bias_fwd, l1_gdn_A_log_bwd, l1_gdn_dt_bias_bwd, l1_gdn_norm, l1_rwkv_mu, l1_rwkv_w0_fwd, l1_rwkv_w2_fwd, l1_rwkv_a0_fwd, l1_rwkv_a2_fwd, l1_rwkv_w0_bwd, l1_rwkv_w2_bwd, l1_rwkv_a0_bwd, l1_rwkv_a2_bwd, l1_rwkv_g2, l1_rwkv_k_k, l1_rwkv_k_a, l1_rwkv_r_k, l1_rwkv_ln_w, l1_rwkv_ln_b, final_norm):
    bp, tp, d = x_prompt.shape
    bs, ts, _ = x_sample.shape
    cond = jnp.concatenate([c_ctx[None, :], c, jnp.zeros((8 - 1 - bs, d), F32)], axis=0)
    common = (
        (l0_w_mod, l0_b_mod, (l0_norm1, l0_norm2, l0_norm3), (l0_ffn1_wg, l0_ffn1_wu, l0_ffn1_wd),
         (l0_ffn2_wg, l0_ffn2_wu, l0_ffn2_wd), l0_w_in, l0_w_out),
        (l1_w_mod, l1_b_mod, (l1_norm1, l1_norm2, l1_norm3), (l1_ffn1_wg, l1_ffn1_wu, l1_ffn1_wd),
         (l1_ffn2_wg, l1_ffn2_wu, l1_ffn2_wd), l1_w_in, l1_w_out),
    )
    caches = (
        (state_l0_gla_fwd, state_l0_gla_bwd, state_l0_ret_fwd, state_l0_ret_bwd),
        (state_l1_gdn_fwd, state_l1_gdn_bwd, state_l1_rwkv_fwd, state_l1_rwkv_bwd),
    )
    l0_pieces = [(0, 2 * GLA_QK + 2 * GLA_V), (L0_END - 2 * RET_QK - 2 * RET_V, L0_END),
                 (2 * GLA_QK + 2 * GLA_V, 2 * GLA_QK + 2 * GLA_V + 2 * GLA_LOWRANK)]
    l0_pieces.append((-L0_END) % PROJ_TN)
    l1_pieces = [(0, GDN_QKV + GDN_V), (GDN_IN + 3 * RWKV_C, GDN_IN + RWKV_IN), (GDN_QKV + GDN_V, GDN_IN),
                 PROJ_TN - RWKV_LORA_IN - 4 * GDN_HEADS, (GDN_IN, GDN_IN + 3 * RWKV_C)]
    w_in_perm = (_permuted_w_in(l0_w_in, l0_pieces), _permuted_w_in(l1_w_in, l1_pieces))
    l0_params = (l0_gla_gk_up_fwd, l0_gla_gk_b_fwd, l0_gla_gk_up_bwd, l0_gla_gk_b_bwd, l0_gla_norm, l0_ret_norm)
    l1_params = (l1_gdn_conv, l1_gdn_A_log_fwd, l1_gdn_dt_bias_fwd, l1_gdn_A_log_bwd, l1_gdn_dt_bias_bwd,
                 l1_gdn_norm, l1_rwkv_mu, l1_rwkv_w0_fwd, l1_rwkv_w2_fwd, l1_rwkv_a0_fwd, l1_rwkv_a2_fwd,
                 l1_rwkv_w0_bwd, l1_rwkv_w2_bwd, l1_rwkv_a0_bwd, l1_rwkv_a2_bwd,
                 l1_rwkv_g2, l1_rwkv_k_k, l1_rwkv_k_a, l1_rwkv_r_k, l1_rwkv_ln_w, l1_rwkv_ln_b)
    bf = lambda w: w.astype(BF16)
    xs = [x_prompt.reshape(bp * tp, d), x_sample.reshape(bs * ts, d)]
    geom = [(bp, tp, bp * tp, slice(0, 1)), (bs, ts, ts, slice(1, 1 + bs))]
    new_states = []
    for layer in range(2):
        w_mod, b_mod, norms, ffn1, ffn2, _, w_out = common[layer]
        m = _adaln(cond, w_mod, b_mod)
        mods = [t_.reshape(8, 1, d) for t_ in jnp.split(m, N_MOD, axis=-1)]
        ffn1 = _ffn_weights(*ffn1)
        ffn2 = _ffn_weights(*ffn2)
        w_out = bf(w_out)
        for gi in range(2):
            b, t, tpm, rows = geom[gi]
            sh1, sc1, g1, sh2, sc2, g2, sh3, sc3, g3 = [mm[rows] for mm in mods]
            x = _ffn(xs[gi], (sh1, sc1, g1), tpm, norms[0], *ffn1, final_norm, False)
            z = _proj_in(x, sh2, sc2, tpm, norms[1], w_in_perm[layer])
            cache = caches[layer] if gi == 1 else None
            if layer == 0:
                x, st = _mixer_gla_ret(x, z, b, t, cache, gi == 1, g2, tpm, w_out, *l0_params)
            else:
                x, st = _mixer_gdn_rwkv(x, z, b, t, cache, g2, tpm, w_out, *l1_params)
            if gi == 0:
                new_states.extend(st)
            xs[gi] = _ffn(x, (sh3, sc3, g3), tpm, norms[2], *ffn2, final_norm, layer == 1)
    return (xs[0].reshape(bp, tp, d), xs[1].reshape(bs, ts, d), *new_states)
```
